```python
import math
import jax, jax.numpy as jnp
from jax import lax
import numpy as np

D_MODEL = 1024
BATCH = 8
SEQ = 4096
DEPTH = 2

M_HEADS = 4
M_DH = D_MODEL // M_HEADS
M_W = M_HEADS * M_DH
M_CHUNK = 128
CONV_W = 4
POOL_WINDOWS = (2, 4, 8, 16)
POOL_GROUPS = 4
POOL_W = D_MODEL
POOL_G = POOL_W // POOL_GROUPS
A_HEADS = 8
A_DK = 64
A_DV = 2 * A_DK
A_QK_W = A_HEADS * 2 * A_DK
A_V_W = A_HEADS * A_DV
Q_BLOCK = 128
ROPE_THETA = 10000.0
N_BRANCH = 3
FF = -(-8 * D_MODEL // (3 * 256)) * 256
NORM_EPS = 1e-6
NEG_BIG = -1e30

SPLIT_SIZES = (M_W, M_W, M_W, M_W, M_HEADS, M_HEADS, POOL_W, A_QK_W, A_QK_W, A_V_W, N_BRANCH * D_MODEL)
N_IN = sum(SPLIT_SIZES)

kernel_name = "hybrid_mlstm_pool_diffattn_block"


def rmsnorm(x, g):
    xf = x.astype(jnp.float32)
    y = xf * lax.rsqrt(jnp.mean(xf * xf, axis=-1, keepdims=True) + NORM_EPS)
    return (y * g.astype(jnp.float32)).astype(x.dtype)


def rope(x):
    S, d = x.shape[-2], x.shape[-1]
    half = d // 2
    inv = ROPE_THETA ** (-jnp.arange(half, dtype=jnp.float32) / half)
    ang = jnp.arange(S, dtype=jnp.float32)[:, None] * inv[None, :]
    cos, sin = jnp.cos(ang), jnp.sin(ang)
    xf = x.astype(jnp.float32)
    x1, x2 = xf[..., :half], xf[..., half:]
    return jnp.concatenate([x1 * cos - x2 * sin, x2 * cos + x1 * sin], axis=-1).astype(x.dtype)


def causal_conv(u, w):
    C = u.shape[-1]
    return lax.conv_general_dilated(u, w[:, None, :].astype(u.dtype), window_strides=(1,),
                                    padding=((CONV_W - 1, 0),),
                                    dimension_numbers=('NWC', 'WIO', 'NWC'),
                                    feature_group_count=C)


def mlstm_chunkwise(q, k, v, i_pre, f_pre):
    B, H, S, Dh = q.shape
    L = M_CHUNK
    NC = S // L
    q = q.reshape(B, H, NC, L, Dh) * (Dh ** -0.5)
    k = k.reshape(B, H, NC, L, Dh)
    v = v.reshape(B, H, NC, L, Dh)
    li = i_pre.reshape(B, H, NC, L)
    lf = jax.nn.log_sigmoid(f_pre).reshape(B, H, NC, L)
    b = jnp.cumsum(lf, axis=-1)
    g = b[..., -1]
    w_state = g[..., None] - b + li
    m_loc = jnp.max(w_state, axis=-1)
    e_state = jnp.exp(w_state - m_loc[..., None])
    C_loc = jnp.einsum('bhcsk,bhcsv->bhckv', k * e_state[..., None], v)
    n_loc = jnp.einsum('bhcs,bhcsk->bhck', e_state, k)

    def step(carry, inp):
        C, n, m = carry
        g_c, m_c, C_c, n_c = inp
        m_new = jnp.maximum(g_c + m, m_c)
        a = jnp.exp(g_c + m - m_new)
        bb = jnp.exp(m_c - m_new)
        C_new = a[..., None, None] * C + bb[..., None, None] * C_c
        n_new = a[..., None] * n + bb[..., None] * n_c
        return (C_new, n_new, m_new), (C, n, m)

    init = (jnp.zeros((B, H, Dh, Dh), jnp.float32), jnp.zeros((B, H, Dh), jnp.float32),
            jnp.full((B, H), NEG_BIG, jnp.float32))
    xs = (jnp.moveaxis(g, 2, 0), jnp.moveaxis(m_loc, 2, 0), jnp.moveaxis(C_loc, 2, 0), jnp.moveaxis(n_loc, 2, 0))
    _, (C_prev, n_prev, m_prev) = lax.scan(step, init, xs)
    C_prev = jnp.moveaxis(C_prev, 0, 2)
    n_prev = jnp.moveaxis(n_prev, 0, 2)
    m_prev = jnp.moveaxis(m_prev, 0, 2)

    causal = jnp.tril(jnp.ones((L, L), dtype=bool))
    Dm = jnp.where(causal, b[..., :, None] - b[..., None, :] + li[..., None, :], -jnp.inf)
    m_inter = b + m_prev[..., None]
    m_t = jnp.maximum(m_inter, jnp.max(Dm, axis=-1))
    P = jnp.exp(Dm - m_t[..., None])
    Sqk = jnp.einsum('bhcjd,bhcsd->bhcjs', q, k) * P
    inter = jnp.exp(m_inter - m_t)
    num = jnp.einsum('bhcjs,bhcsv->bhcjv', Sqk, v) + inter[..., None] * jnp.einsum('bhcjk,bhckv->bhcjv', q, C_prev)
    den = jnp.sum(Sqk, axis=-1) + inter * jnp.einsum('bhcjk,bhck->bhcj', q, n_prev)
    h = num / jnp.maximum(jnp.abs(den), jnp.exp(-m_t))[..., None]
    return h.reshape(B, H, S, Dh)


def pool_mixer(u, w_pool, scale):
    B, S, _ = u.shape
    uf = u.astype(jnp.float32)
    cs = jnp.concatenate([jnp.zeros((B, 1, POOL_W), jnp.float32), jnp.cumsum(uf, axis=1)], axis=1)
    pos1 = jnp.arange(1, S + 1)
    means = []
    for gi, w in enumerate(POOL_WINDOWS):
        c = cs[..., gi * POOL_G:(gi + 1) * POOL_G]
        lo = jnp.concatenate([jnp.zeros((B, w - 1, POOL_G), jnp.float32), c[:, :S - w + 1]], axis=1)
        cnt = jnp.minimum(pos1, w).astype(jnp.float32)
        means.append((c[:, 1:] - lo) / cnt[None, :, None])
    pooled = jnp.concatenate(means, axis=-1) - uf
    y = jnp.einsum('bsgi,gio->bsgo', pooled.reshape(B, S, POOL_GROUPS, POOL_G),
                   w_pool.astype(jnp.float32)).reshape(B, S, POOL_W)
    return (y * scale.astype(jnp.float32)).astype(u.dtype)


def diff_attention(q, k, v, lam):
    B, H, _, S, dk = q.shape
    nb = S // Q_BLOCK
    qb = jnp.moveaxis(q.reshape(B, H, 2, nb, Q_BLOCK, dk), 3, 0)
    kpos = jnp.arange(S)
    scale = dk ** -0.5

    def block(args):
        q_blk, blk = args
        s = jnp.einsum('bhcqd,bhckd->bhcqk', q_blk, k).astype(jnp.float32) * scale
        qpos = blk * Q_BLOCK + jnp.arange(Q_BLOCK)
        s = jnp.where(kpos[None, :] <= qpos[:, None], s, -jnp.inf)
        p = jax.nn.softmax(s, axis=-1)
        a = p[:, :, 0] - lam * p[:, :, 1]
        return jnp.einsum('bhqk,bhkd->bhqd', a, v.astype(jnp.float32))

    out = lax.map(block, (qb, jnp.arange(nb)))
    return jnp.moveaxis(out, 0, 2).reshape(B, H, S, v.shape[-1]).astype(v.dtype)


def hybrid_mixer(x, layer_idx, g_mix, w_in, b_if, conv_qk, w_m_out, w_pool, pool_scale,
                 g_qk, lam_p, g_diff_head, w_diff_out, w_out):
    B, S, _ = x.shape
    h = rmsnorm(x, g_mix)
    proj = h @ w_in
    mq, mk, mv, mo, mi, mf, pu, aq, ak, av, gate_pre = jnp.split(
        proj, np.cumsum(SPLIT_SIZES)[:-1].tolist(), axis=-1)

    qk = jax.nn.silu(causal_conv(jnp.concatenate([mq, mk], axis=-1), conv_qk))
    mq, mk = jnp.split(qk, 2, axis=-1)

    def heads(t):
        return t.reshape(B, S, M_HEADS, M_DH).transpose(0, 2, 1, 3).astype(jnp.float32)

    i_pre = (mi + b_if[:M_HEADS]).astype(jnp.float32).transpose(0, 2, 1)
    f_pre = (mf + b_if[M_HEADS:]).astype(jnp.float32).transpose(0, 2, 1)
    hm = mlstm_chunkwise(heads(mq), heads(mk), heads(mv), i_pre, f_pre)
    hm = hm.transpose(0, 2, 1, 3).reshape(B, S, M_W).astype(x.dtype)
    y_m = (jax.nn.sigmoid(mo) * hm) @ w_m_out

    y_p = pool_mixer(pu, w_pool, pool_scale)

    q = aq.reshape(B, S, A_HEADS, 2, A_DK).transpose(0, 2, 3, 1, 4)
    k = ak.reshape(B, S, A_HEADS, 2, A_DK).transpose(0, 2, 3, 1, 4)
    q = rope(rmsnorm(q, g_qk[0]))
    k = rope(rmsnorm(k, g_qk[1]))
    v = av.reshape(B, S, A_HEADS, A_DV).transpose(0, 2, 1, 3)
    lam_init = 0.8 - 0.6 * math.exp(-0.3 * layer_idx)
    lp = lam_p.astype(jnp.float32)
    lam = jnp.exp(jnp.sum(lp[0] * lp[1])) - jnp.exp(jnp.sum(lp[2] * lp[3])) + lam_init
    o = diff_attention(q, k, v, lam)
    o = rmsnorm(o, g_diff_head) * (1.0 - lam_init)
    y_a = o.transpose(0, 2, 1, 3).reshape(B, S, A_V_W) @ w_diff_out

    g_m, g_p, g_a = jnp.split(jax.nn.sigmoid(gate_pre), N_BRANCH, axis=-1)
    merged = g_m * y_m + g_p * y_p + g_a * y_a
    return merged @ w_out


def swiglu_ffn(x, g_ffn, w_gate_up, w_down):
    h = rmsnorm(x, g_ffn)
    gate, up = jnp.split(h @ w_gate_up, 2, axis=-1)
    return (jax.nn.silu(gate) * up) @ w_down


def setup_inputs(seed: int = 0) -> dict:
    key = jax.random.key(seed)
    ks = jax.random.split(key, 18)
    f32 = jnp.float32

    def nrm(k, shape, scale):
        return jax.random.normal(k, shape, f32) * scale

    x = nrm(ks[0], (BATCH, SEQ, D_MODEL), 1.0)
    g_mix = 1.0 + nrm(ks[1], (DEPTH, D_MODEL), 0.02)
    w_in = nrm(ks[2], (DEPTH, D_MODEL, N_IN), D_MODEL ** -0.5)
    b_i = nrm(ks[3], (DEPTH, M_HEADS), 0.1)
    b_f = jnp.linspace(3.0, 6.0, M_HEADS, dtype=f32)[None, :] + nrm(ks[4], (DEPTH, M_HEADS), 0.1)
    b_if = jnp.concatenate([b_i, b_f], axis=-1)
    conv_qk = nrm(ks[5], (DEPTH, CONV_W, 2 * M_W), CONV_W ** -0.5)
    w_m_out = nrm(ks[6], (DEPTH, M_W, D_MODEL), M_W ** -0.5)
    w_pool = nrm(ks[7], (DEPTH, POOL_GROUPS, POOL_G, POOL_G), POOL_G ** -0.5)
    pool_scale = 1.0 + nrm(ks[8], (DEPTH, POOL_W), 0.02)
    g_qk = 1.0 + nrm(ks[9], (DEPTH, 2, A_DK), 0.02)
    lam_p = nrm(ks[10], (DEPTH, 4, A_DK), 0.1)
    g_diff_head = 1.0 + nrm(ks[11], (DEPTH, A_DV), 0.02)
    w_diff_out = nrm(ks[12], (DEPTH, A_V_W, D_MODEL), A_V_W ** -0.5)
    w_out = nrm(ks[13], (DEPTH, D_MODEL, D_MODEL), D_MODEL ** -0.5)
    g_ffn = 1.0 + nrm(ks[14], (DEPTH, D_MODEL), 0.02)
    w_gate_up = nrm(ks[15], (DEPTH, D_MODEL, 2 * FF), D_MODEL ** -0.5)
    w_down = nrm(ks[16], (DEPTH, FF, D_MODEL), FF ** -0.5)
    return {"x": x, "g_mix": g_mix, "w_in": w_in, "b_if": b_if, "conv_qk": conv_qk,
            "w_m_out": w_m_out, "w_pool": w_pool, "pool_scale": pool_scale, "g_qk": g_qk,
            "lam_p": lam_p, "g_diff_head": g_diff_head, "w_diff_out": w_diff_out, "w_out": w_out,
            "g_ffn": g_ffn, "w_gate_up": w_gate_up, "w_down": w_down}


def reference(x, g_mix, w_in, b_if, conv_qk, w_m_out, w_pool, pool_scale, g_qk, lam_p,
              g_diff_head, w_diff_out, w_out, g_ffn, w_gate_up, w_down):
    for l in range(DEPTH):
        x = x + hybrid_mixer(x, l, g_mix[l], w_in[l], b_if[l], conv_qk[l], w_m_out[l], w_pool[l],
                             pool_scale[l], g_qk[l], lam_p[l], g_diff_head[l], w_diff_out[l], w_out[l])
        x = x + swiglu_ffn(x, g_ffn[l], w_gate_up[l], w_down[l])
    return x
```

```python
import functools
import math

import jax
import jax.numpy as jnp
from jax import lax
from jax.experimental import pallas as pl
from jax.experimental.pallas import tpu as pltpu

D_MODEL = 1024
DEPTH = 2
M_HEADS = 4
M_DH = 256
M_CHUNK = 128
CONV_W = 4
POOL_WINDOWS = (2, 4, 8, 16)
POOL_G = 256
POOL_HALO = 16
A_HEADS = 8
A_DK = 64
A_DV = 128
FF = 2816
NORM_EPS = 1e-6
NEG_BIG = -1e30
ROPE_THETA = 10000.0

COL_MQ, COL_MK, COL_MV, COL_MO = 0, 1024, 2048, 3072
COL_PU, COL_AQ, COL_AK, COL_AV, COL_GATE = 4096, 5120, 6144, 7168, 8192
N_MAIN = 11264
GATE_ROWS = 16

BF16 = jnp.bfloat16
F32 = jnp.float32

VMEM_LIMIT = 56 * 1024 * 1024

PROJ_TM, PROJ_TN = 1024, 1024
QKPREP_TM = 512
ATTN_T = 512
MERGE_TM = 256
FFN_TM = 256
FFN_CHUNKS = ((0, 512), (512, 1024), (1024, 1536), (1536, 2048), (2048, 2560), (2560, 2816))


def _params(n_axes):
    return pltpu.CompilerParams(dimension_semantics=("arbitrary",) * n_axes,
                                vmem_limit_bytes=VMEM_LIMIT)


def _sigmoid(x):
    return 1.0 / (1.0 + jnp.exp(-x))


def _proj_kernel(x_ref, g_ref, w_ref, wif_ref, out_ref, gates_ref, h_ref):
    @pl.when(pl.program_id(1) == 0)
    def _():
        x = x_ref[...]
        ms = jnp.mean(x * x, axis=-1, keepdims=True)
        hb = (x * lax.rsqrt(ms + NORM_EPS) * g_ref[...]).astype(BF16)
        h_ref[...] = hb
        gates_ref[...] = lax.dot_general(wif_ref[...], hb, (((1,), (1,)), ((), ())),
                                         preferred_element_type=F32)

    out_ref[...] = jnp.dot(h_ref[...], w_ref[...], preferred_element_type=F32).astype(BF16)


def _proj(xf, g, w_main, wif_t):
    T = xf.shape[0]
    tm, tn = PROJ_TM, PROJ_TN
    return pl.pallas_call(
        _proj_kernel,
        grid=(T // tm, N_MAIN // tn),
        in_specs=[pl.BlockSpec((tm, D_MODEL), lambda i, j: (i, 0)),
                  pl.BlockSpec((1, D_MODEL), lambda i, j: (0, 0)),
                  pl.BlockSpec((D_MODEL, tn), lambda i, j: (0, j)),
                  pl.BlockSpec((GATE_ROWS, D_MODEL), lambda i, j: (0, 0))],
        out_specs=[pl.BlockSpec((tm, tn), lambda i, j: (i, j)),
                   pl.BlockSpec((GATE_ROWS, tm), lambda i, j: (0, i))],
        out_shape=[jax.ShapeDtypeStruct((T, N_MAIN), BF16),
                   jax.ShapeDtypeStruct((GATE_ROWS, T), F32)],
        scratch_shapes=[pltpu.VMEM((tm, D_MODEL), BF16)],
        compiler_params=_params(2),
        name="proj",
    )(xf, g, w_main, wif_t)


def _mlstm_kernel(q_ref, k_ref, v_ref, o_ref, gt_ref, bif_ref, cw_ref, out_ref,
                  ubuf, c_ref, n_ref, m_ref):
    L, H, Dh = M_CHUNK, M_HEADS, M_DH

    @pl.when(pl.program_id(1) == 0)
    def _():
        ubuf[0:8, :] = jnp.zeros((8, 2 * H * Dh), F32)
        c_ref[...] = jnp.zeros(c_ref.shape, F32)
        n_ref[...] = jnp.zeros(n_ref.shape, F32)
        m_ref[...] = jnp.full(m_ref.shape, NEG_BIG, F32)

    ubuf[8:8 + L, 0:H * Dh] = q_ref[...].astype(F32)
    ubuf[8:8 + L, H * Dh:2 * H * Dh] = k_ref[...].astype(F32)

    gt = gt_ref[...] + bif_ref[...]
    lf = jnp.minimum(gt, 0.0) - jnp.log(1.0 + jnp.exp(-jnp.abs(gt)))
    lane = lax.broadcasted_iota(jnp.int32, (GATE_ROWS, L), 1)
    b_all = lf
    for sh in (1, 2, 4, 8, 16, 32, 64):
        b_all = b_all + jnp.where(lane >= sh, pltpu.roll(b_all, sh, axis=1), 0.0)

    row = lax.broadcasted_iota(jnp.int32, (L, L), 0)
    col = lax.broadcasted_iota(jnp.int32, (L, L), 1)
    eye = row == col
    tril = row >= col

    def to_col(r):
        return jnp.sum(jnp.where(eye, jnp.broadcast_to(r, (L, L)), 0.0), axis=1, keepdims=True)

    def conv_silu(c0):
        cols = slice(c0, c0 + Dh)
        y = (cw_ref[3:4, cols] * ubuf[8:8 + L, cols] + cw_ref[2:3, cols] * ubuf[7:7 + L, cols]
             + cw_ref[1:2, cols] * ubuf[6:6 + L, cols] + cw_ref[0:1, cols] * ubuf[5:5 + L, cols])
        return y * _sigmoid(y)

    for h in range(H):
        qh = conv_silu(h * Dh) * (Dh ** -0.5)
        kh = conv_silu(H * Dh + h * Dh)
        vb = v_ref[:, h * Dh:(h + 1) * Dh]
        qb = qh.astype(BF16)
        kb = kh.astype(BF16)

        li = gt[h:h + 1, :]
        b_row = b_all[H + h:H + h + 1, :]
        g_tot = b_row[:, L - 1:L]
        rowterm = li - b_row
        w_state = g_tot + rowterm
        m_loc = jnp.max(w_state, axis=1, keepdims=True)
        e_col = to_col(jnp.exp(w_state - m_loc))
        b_col = to_col(b_row)

        c_prev = c_ref[h]
        n_prev = n_ref[h:h + 1, :]
        m_prev = m_ref[h:h + 1, 0:1]

        dm = jnp.where(tril, b_col + rowterm, NEG_BIG)
        m_intra = jnp.max(dm, axis=1, keepdims=True)
        m_inter = b_col + m_prev
        m_t = jnp.maximum(m_inter, m_intra)
        p = jnp.exp(dm - m_t)
        sqk = lax.dot_general(qb, kb, (((1,), (1,)), ((), ())), preferred_element_type=F32) * p
        inter = jnp.exp(m_inter - m_t)
        num = (jnp.dot(sqk.astype(BF16), vb, preferred_element_type=F32)
               + inter * jnp.dot(qb, c_prev.astype(BF16), preferred_element_type=F32))
        den = (jnp.sum(sqk, axis=1, keepdims=True)
               + inter * jnp.sum(qh * n_prev, axis=1, keepdims=True))
        hh = num / jnp.maximum(jnp.abs(den), jnp.exp(-m_t))
        og = o_ref[:, h * Dh:(h + 1) * Dh].astype(F32)
        out_ref[:, h * Dh:(h + 1) * Dh] = (_sigmoid(og) * hh).astype(BF16)

        ke = kh * e_col
        c_loc = lax.dot_general(ke.astype(BF16), vb, (((0,), (0,)), ((), ())),
                                preferred_element_type=F32)
        n_loc = jnp.sum(ke, axis=0, keepdims=True)
        m_new = jnp.maximum(g_tot + m_prev, m_loc)
        a = jnp.exp(g_tot + m_prev - m_new)
        bb = jnp.exp(m_loc - m_new)
        c_ref[h] = a * c_prev + bb * c_loc
        n_ref[h:h + 1, :] = a * n_prev + bb * n_loc
        m_ref[h:h + 1, :] = jnp.broadcast_to(m_new, (1, 128))

    ubuf[0:8, :] = ubuf[L:L + 8, :]


def _mlstm(P, gates, bif, conv_w, B, S):
    T = B * S
    L, H, Dh = M_CHUNK, M_HEADS, M_DH
    nc = S // L
    W = H * Dh
    blk = lambda cb: pl.BlockSpec((L, W), lambda b, c: (b * nc + c, cb))
    return pl.pallas_call(
        _mlstm_kernel,
        grid=(B, nc),
        in_specs=[blk(COL_MQ // W), blk(COL_MK // W), blk(COL_MV // W), blk(COL_MO // W),
                  pl.BlockSpec((GATE_ROWS, L), lambda b, c: (0, b * nc + c)),
                  pl.BlockSpec((GATE_ROWS, 1), lambda b, c: (0, 0)),
                  pl.BlockSpec((CONV_W, 2 * W), lambda b, c: (0, 0))],
        out_specs=pl.BlockSpec((L, W), lambda b, c: (b * nc + c, 0)),
        out_shape=jax.ShapeDtypeStruct((T, W), BF16),
        scratch_shapes=[pltpu.VMEM((L + 8, 2 * W), F32),
                        pltpu.VMEM((H, Dh, Dh), F32),
                        pltpu.VMEM((8, Dh), F32),
                        pltpu.VMEM((8, 128), F32)],
        compiler_params=_params(2),
        name="mlstm",
    )(P, P, P, P, gates, bif, conv_w)


def _qkprep_kernel(aq_ref, ak_ref, cos_ref, sin_ref, g_ref, gm_ref, q_out, k_out):
    tm = aq_ref.shape[0]
    cos = cos_ref[...]
    sin = sin_ref[...]
    lane = lax.broadcasted_iota(jnp.int32, (tm, 128), 1)
    first_half = (lane & (A_DK // 2)) == 0
    gm = gm_ref[...]
    for src, dst, gi, scale in ((aq_ref, q_out, 0, A_DK ** -0.5), (ak_ref, k_out, 1, 1.0)):
        g = g_ref[gi:gi + 1, :] * scale
        for c in range(A_HEADS):
            cols = slice(c * 128, (c + 1) * 128)
            x = src[:, cols].astype(F32)
            x2 = x * x
            hi = x2.astype(BF16)
            lo = (x2 - hi.astype(F32)).astype(BF16)
            ss = (jnp.dot(hi, gm, preferred_element_type=F32)
                  + jnp.dot(lo, gm, preferred_element_type=F32))
            y = x * lax.rsqrt(ss * (1.0 / A_DK) + NORM_EPS) * g
            partner = jnp.where(first_half, pltpu.roll(y, 128 - A_DK // 2, axis=1),
                                pltpu.roll(y, A_DK // 2, axis=1))
            dst[:, cols] = (y * cos + partner * sin).astype(BF16)


def _qkprep(P, cos_t, sin_t, g2, gmat, B, S):
    T = B * S
    tm = QKPREP_TM
    W = A_HEADS * 2 * A_DK
    spb = S // tm
    return pl.pallas_call(
        _qkprep_kernel,
        grid=(T // tm,),
        in_specs=[pl.BlockSpec((tm, W), lambda i: (i, COL_AQ // W)),
                  pl.BlockSpec((tm, W), lambda i: (i, COL_AK // W)),
                  pl.BlockSpec((tm, 128), lambda i: (i % spb, 0)),
                  pl.BlockSpec((tm, 128), lambda i: (i % spb, 0)),
                  pl.BlockSpec((2, 128), lambda i: (0, 0)),
                  pl.BlockSpec((128, 128), lambda i: (0, 0))],
        out_specs=[pl.BlockSpec((tm, W), lambda i: (i, 0)),
                   pl.BlockSpec((tm, W), lambda i: (i, 0))],
        out_shape=[jax.ShapeDtypeStruct((T, W), BF16), jax.ShapeDtypeStruct((T, W), BF16)],
        compiler_params=_params(1),
        name="qkprep",
    )(P, P, cos_t, sin_t, g2, gmat)


def _attn_kernel(q_ref, k_ref, v_ref, lam_ref, gd_ref, out_ref, qs_ref, m_ref, l_ref, acc_ref,
                 *, lam_init):
    t = q_ref.shape[0]
    qi = pl.program_id(2)

    q = q_ref[...]
    lane = lax.broadcasted_iota(jnp.int32, (t, 128), 1)
    zero = jnp.zeros((t, 128), BF16)
    qs_ref[0:t, :] = jnp.where(lane < A_DK, q, zero)
    qs_ref[t:2 * t, :] = jnp.where(lane >= A_DK, q, zero)
    m_ref[...] = jnp.full(m_ref.shape, NEG_BIG, F32)
    l_ref[...] = jnp.zeros(l_ref.shape, F32)
    acc_ref[...] = jnp.zeros(acc_ref.shape, F32)

    def step(j, masked):
        start = pl.multiple_of(j * t, t)
        k = k_ref[pl.ds(start, t), :]
        v = v_ref[pl.ds(start, t), :]
        s = lax.dot_general(qs_ref[...], k, (((1,), (1,)), ((), ())), preferred_element_type=F32)
        if masked:
            row = lax.broadcasted_iota(jnp.int32, (2 * t, t), 0)
            col = lax.broadcasted_iota(jnp.int32, (2 * t, t), 1)
            qrow = jnp.where(row >= t, row - t, row)
            s = jnp.where(col <= qrow, s, NEG_BIG)
        m_prev = m_ref[...]
        m_new = jnp.maximum(m_prev, jnp.max(s, axis=1, keepdims=True))
        alpha = jnp.exp(m_prev - m_new)
        p = jnp.exp(s - m_new[:, 0:1])
        l_ref[...] = alpha * l_ref[...] + jnp.sum(p, axis=1, keepdims=True)
        acc_ref[...] = alpha * acc_ref[...] + jnp.dot(p.astype(BF16), v, preferred_element_type=F32)
        m_ref[...] = m_new

    def body(j, carry):
        step(j, False)
        return carry

    lax.fori_loop(0, qi, body, 0)
    step(qi, True)

    o1 = acc_ref[0:t, :] / l_ref[0:t, :]
    o2 = acc_ref[t:2 * t, :] / l_ref[t:2 * t, :]
    lp = lam_ref[...]
    lam = (jnp.exp(jnp.sum(lp[0:1, :] * lp[1:2, :], axis=1, keepdims=True))
           - jnp.exp(jnp.sum(lp[2:3, :] * lp[3:4, :], axis=1, keepdims=True)) + lam_init)
    o = o1 - lam * o2
    ms = jnp.mean(o * o, axis=-1, keepdims=True)
    out_ref[...] = (o * lax.rsqrt(ms + NORM_EPS) * (gd_ref[...] * (1.0 - lam_init))).astype(BF16)


def _attn(qr, kr, P, lam_p, gd, lam_init, B, S):
    T = B * S
    t = ATTN_T
    nq = S // t
    return pl.pallas_call(
        functools.partial(_attn_kernel, lam_init=lam_init),
        grid=(B, A_HEADS, nq),
        in_specs=[pl.BlockSpec((t, 128), lambda b, h, i: (b * nq + i, h)),
                  pl.BlockSpec((S, 128), lambda b, h, i: (b, h)),
                  pl.BlockSpec((S, 128), lambda b, h, i: (b, COL_AV // 128 + h)),
                  pl.BlockSpec((4, A_DK), lambda b, h, i: (0, 0)),
                  pl.BlockSpec((1, A_DV), lambda b, h, i: (0, 0))],
        out_specs=pl.BlockSpec((t, 128), lambda b, h, i: (b * nq + i, h)),
        out_shape=jax.ShapeDtypeStruct((T, A_HEADS * A_DV), BF16),
        scratch_shapes=[pltpu.VMEM((2 * t, 128), BF16),
                        pltpu.VMEM((2 * t, 128), F32),
                        pltpu.VMEM((2 * t, 128), F32),
                        pltpu.VMEM((2 * t, 128), F32)],
        compiler_params=_params(3),
        name="attn",
    )(qr, kr, P, lam_p, gd)


def _merge_kernel(hm_ref, pu_ref, ao_ref, gm_ref, gp_ref, ga_ref, x_ref, wm_ref, wp_ref, ps_ref,
                  wd_ref, wo_ref, out_ref, pbuf, *, tiles_per_seq):
    tm = x_ref.shape[0]
    G = POOL_G
    i = pl.program_id(0)
    tile_in_seq = i % tiles_per_seq

    @pl.when(tile_in_seq == 0)
    def _():
        pbuf[0:POOL_HALO, :] = jnp.zeros((POOL_HALO, D_MODEL), F32)

    pbuf[POOL_HALO:POOL_HALO + tm, :] = pu_ref[...].astype(F32)
    pos1 = tile_in_seq * tm + lax.broadcasted_iota(jnp.int32, (tm, 1), 0) + 1

    yp = []
    for gi, w in enumerate(POOL_WINDOWS):
        cols = slice(gi * G, (gi + 1) * G)
        u = pbuf[POOL_HALO:POOL_HALO + tm, cols]
        acc = u
        for j in range(1, w):
            acc = acc + pbuf[POOL_HALO - j:POOL_HALO - j + tm, cols]
        cnt = jnp.minimum(pos1, w).astype(F32)
        pooled = acc / cnt - u
        yp.append(jnp.dot(pooled.astype(BF16), wp_ref[gi], preferred_element_type=F32))
    pbuf[0:POOL_HALO, :] = pbuf[tm:tm + POOL_HALO, :]

    y_p = jnp.concatenate(yp, axis=1) * ps_ref[...]
    y_m = jnp.dot(hm_ref[...], wm_ref[...], preferred_element_type=F32)
    y_a = jnp.dot(ao_ref[...], wd_ref[...], preferred_element_type=F32)
    g_m = _sigmoid(gm_ref[...].astype(F32))
    g_p = _sigmoid(gp_ref[...].astype(F32))
    g_a = _sigmoid(ga_ref[...].astype(F32))
    merged = (g_m * y_m + g_p * y_p + g_a * y_a).astype(BF16)
    out_ref[...] = x_ref[...] + jnp.dot(merged, wo_ref[...], preferred_element_type=F32)


def _merge(hm, P, ao, xf, wm, wp, ps, wd, wo, B, S):
    T = B * S
    tm = MERGE_TM
    row = lambda w, cb: pl.BlockSpec((tm, w), lambda i: (i, cb))
    full = lambda shape: pl.BlockSpec(shape, lambda i: (0,) * len(shape))
    return pl.pallas_call(
        functools.partial(_merge_kernel, tiles_per_seq=S // tm),
        grid=(T // tm,),
        in_specs=[row(D_MODEL, 0), row(D_MODEL, COL_PU // D_MODEL), row(D_MODEL, 0),
                  row(D_MODEL, COL_GATE // D_MODEL), row(D_MODEL, COL_GATE // D_MODEL + 1),
                  row(D_MODEL, COL_GATE // D_MODEL + 2), row(D_MODEL, 0),
                  full((D_MODEL, D_MODEL)), full((4, POOL_G, POOL_G)), full((1, D_MODEL)),
                  full((D_MODEL, D_MODEL)), full((D_MODEL, D_MODEL))],
        out_specs=row(D_MODEL, 0),
        out_shape=jax.ShapeDtypeStruct((T, D_MODEL), F32),
        scratch_shapes=[pltpu.VMEM((tm + POOL_HALO, D_MODEL), F32)],
        compiler_params=_params(1),
        name="merge",
    )(hm, P, ao, P, P, P, xf, wm, wp, ps, wd, wo)


def _ffn_kernel(x_ref, g_ref, wgu_ref, wdn_ref, out_ref, act_ref):
    x = x_ref[...]
    ms = jnp.mean(x * x, axis=-1, keepdims=True)
    hb = (x * lax.rsqrt(ms + NORM_EPS) * g_ref[...]).astype(BF16)
    for lo, hi in FFN_CHUNKS:
        gate = jnp.dot(hb, wgu_ref[:, lo:hi], preferred_element_type=F32)
        up = jnp.dot(hb, wgu_ref[:, FF + lo:FF + hi], preferred_element_type=F32)
        act_ref[:, lo:hi] = (gate * _sigmoid(gate) * up).astype(BF16)
    out_ref[...] = x + jnp.dot(act_ref[...], wdn_ref[...], preferred_element_type=F32)


def _ffn(xf, g, wgu, wdn):
    T = xf.shape[0]
    tm = FFN_TM
    const = lambda shape: pl.BlockSpec(shape, lambda i: (0, 0), pipeline_mode=pl.Buffered(1))
    return pl.pallas_call(
        _ffn_kernel,
        grid=(T // tm,),
        in_specs=[pl.BlockSpec((tm, D_MODEL), lambda i: (i, 0)),
                  pl.BlockSpec((1, D_MODEL), lambda i: (0, 0)),
                  const((D_MODEL, 2 * FF)), const((FF, D_MODEL))],
        out_specs=pl.BlockSpec((tm, D_MODEL), lambda i: (i, 0)),
        out_shape=jax.ShapeDtypeStruct((T, D_MODEL), F32),
        scratch_shapes=[pltpu.VMEM((tm, FF), BF16)],
        compiler_params=_params(1),
        name="ffn",
    )(xf, g, wgu, wdn)


def _rope_tables(S):
    half = A_DK // 2
    inv = ROPE_THETA ** (-jnp.arange(half, dtype=F32) / half)
    ang = jnp.arange(S, dtype=F32)[:, None] * inv[None, :]
    cos, sin = jnp.cos(ang), jnp.sin(ang)
    cos_t = jnp.concatenate([cos, cos, cos, cos], axis=1)
    sin_t = jnp.concatenate([-sin, sin, -sin, sin], axis=1)
    return cos_t, sin_t


def kernel(x, g_mix, w_in, b_if, conv_qk, w_m_out, w_pool, pool_scale, g_qk, lam_p, g_diff_head,
           w_diff_out, w_out, g_ffn, w_gate_up, w_down):
    B, S, _ = x.shape
    T = B * S
    xf = x.reshape(T, D_MODEL)
    cos_t, sin_t = _rope_tables(S)
    lane = jnp.arange(128)
    gmat = (lane[:, None] // A_DK == lane[None, :] // A_DK).astype(BF16)
    n_gate = 2 * M_HEADS
    gate_lo = 4 * M_HEADS * M_DH

    for l in range(DEPTH):
        wl = w_in[l]
        w_main = jnp.concatenate([wl[:, :gate_lo], wl[:, gate_lo + n_gate:]], axis=1).astype(BF16)
        wif_t = jnp.pad(wl[:, gate_lo:gate_lo + n_gate].T, ((0, GATE_ROWS - n_gate), (0, 0))).astype(BF16)
        bif = jnp.pad(b_if[l], (0, GATE_ROWS - n_gate)).reshape(GATE_ROWS, 1)
        g2 = jnp.concatenate([g_qk[l], g_qk[l]], axis=1)
        lam_init = 0.8 - 0.6 * math.exp(-0.3 * l)

        P, gates = _proj(xf, g_mix[l].reshape(1, D_MODEL), w_main, wif_t)
        hm = _mlstm(P, gates, bif, conv_qk[l], B, S)
        qr, kr = _qkprep(P, cos_t, sin_t, g2, gmat, B, S)
        ao = _attn(qr, kr, P, lam_p[l], g_diff_head[l].reshape(1, A_DV), lam_init, B, S)
        x1 = _merge(hm, P, ao, xf, w_m_out[l].astype(BF16), w_pool[l].astype(BF16),
                    pool_scale[l].reshape(1, D_MODEL), w_diff_out[l].astype(BF16),
                    w_out[l].astype(BF16), B, S)
        xf = _ffn(x1, g_ffn[l].reshape(1, D_MODEL), w_gate_up[l].astype(BF16),
                  w_down[l].astype(BF16))
    return xf.reshape(B, S, D_MODEL)
```

```python
import functools
import math

import jax
import jax.numpy as jnp
from jax import lax
from jax.experimental import pallas as pl
from jax.experimental.pallas import tpu as pltpu

D_MODEL = 1024
DEPTH = 2
M_HEADS = 4
M_DH = 256
M_CHUNK = 128
CONV_W = 4
POOL_WINDOWS = (2, 4, 8, 16)
POOL_G = 256
POOL_HALO = 16
A_HEADS = 8
A_DK = 64
A_DV = 128
FF = 2816
NORM_EPS = 1e-6
NEG_BIG = -1e30
ROPE_THETA = 10000.0

COL_MQ, COL_MK, COL_MV, COL_MO = 0, 1024, 2048, 3072
COL_PU, COL_AQ, COL_AK, COL_AV, COL_GATE = 4096, 5120, 6144, 7168, 8192
N_MAIN = 11264
GATE_ROWS = 16

BF16 = jnp.bfloat16
F32 = jnp.float32

VMEM_LIMIT = 56 * 1024 * 1024

PROJ_TM, PROJ_TN = 1024, 1024
QKPREP_TM = 512
ATTN_T = 512
ATTN_MAX_STATIC_BOUND = 60.0
MERGE_TM = 256
FFN_TM = 256
FFN_CHUNKS = ((0, 512), (512, 1024), (1024, 1536), (1536, 2048), (2048, 2560), (2560, 2816))


def _params(n_axes):
    return pltpu.CompilerParams(dimension_semantics=("arbitrary",) * n_axes,
                                vmem_limit_bytes=VMEM_LIMIT)


def _sigmoid(x):
    return 1.0 / (1.0 + jnp.exp(-x))


def _proj_kernel(x_ref, g_ref, w_ref, wif_ref, out_ref, gates_ref, h_ref):
    @pl.when(pl.program_id(1) == 0)
    def _():
        x = x_ref[...]
        ms = jnp.mean(x * x, axis=-1, keepdims=True)
        hb = (x * lax.rsqrt(ms + NORM_EPS) * g_ref[...]).astype(BF16)
        h_ref[...] = hb
        gates_ref[...] = lax.dot_general(wif_ref[...], hb, (((1,), (1,)), ((), ())),
                                         preferred_element_type=F32)

    out_ref[...] = jnp.dot(h_ref[...], w_ref[...], preferred_element_type=F32).astype(BF16)


def _proj(xf, g, w_main, wif_t):
    T = xf.shape[0]
    tm, tn = PROJ_TM, PROJ_TN
    return pl.pallas_call(
        _proj_kernel,
        grid=(T // tm, N_MAIN // tn),
        in_specs=[pl.BlockSpec((tm, D_MODEL), lambda i, j: (i, 0)),
                  pl.BlockSpec((1, D_MODEL), lambda i, j: (0, 0)),
                  pl.BlockSpec((D_MODEL, tn), lambda i, j: (0, j)),
                  pl.BlockSpec((GATE_ROWS, D_MODEL), lambda i, j: (0, 0))],
        out_specs=[pl.BlockSpec((tm, tn), lambda i, j: (i, j)),
                   pl.BlockSpec((GATE_ROWS, tm), lambda i, j: (0, i))],
        out_shape=[jax.ShapeDtypeStruct((T, N_MAIN), BF16),
                   jax.ShapeDtypeStruct((GATE_ROWS, T), F32)],
        scratch_shapes=[pltpu.VMEM((tm, D_MODEL), BF16)],
        compiler_params=_params(2),
        name="proj",
    )(xf, g, w_main, wif_t)


def _mlstm_kernel(q_ref, k_ref, v_ref, o_ref, gt_ref, bif_ref, cw_ref, out_ref,
                  ubuf, c_ref, n_ref, m_ref):
    L, H, Dh = M_CHUNK, M_HEADS, M_DH

    @pl.when(pl.program_id(1) == 0)
    def _():
        ubuf[0:8, :] = jnp.zeros((8, 2 * H * Dh), F32)
        c_ref[...] = jnp.zeros(c_ref.shape, F32)
        n_ref[...] = jnp.zeros(n_ref.shape, F32)
        m_ref[...] = jnp.full(m_ref.shape, NEG_BIG, F32)

    ubuf[8:8 + L, 0:H * Dh] = q_ref[...].astype(F32)
    ubuf[8:8 + L, H * Dh:2 * H * Dh] = k_ref[...].astype(F32)

    gt = gt_ref[...] + bif_ref[...]
    lf = jnp.minimum(gt, 0.0) - jnp.log(1.0 + jnp.exp(-jnp.abs(gt)))
    lane = lax.broadcasted_iota(jnp.int32, (GATE_ROWS, L), 1)
    b_all = lf
    for sh in (1, 2, 4, 8, 16, 32, 64):
        b_all = b_all + jnp.where(lane >= sh, pltpu.roll(b_all, sh, axis=1), 0.0)

    row = lax.broadcasted_iota(jnp.int32, (L, L), 0)
    col = lax.broadcasted_iota(jnp.int32, (L, L), 1)
    eye = row == col
    tril = row >= col

    def to_col(r):
        return jnp.sum(jnp.where(eye, jnp.broadcast_to(r, (L, L)), 0.0), axis=1, keepdims=True)

    def conv_silu(c0):
        cols = slice(c0, c0 + Dh)
        y = (cw_ref[3:4, cols] * ubuf[8:8 + L, cols] + cw_ref[2:3, cols] * ubuf[7:7 + L, cols]
             + cw_ref[1:2, cols] * ubuf[6:6 + L, cols] + cw_ref[0:1, cols] * ubuf[5:5 + L, cols])
        return y * _sigmoid(y)

    for h in range(H):
        qh = conv_silu(h * Dh) * (Dh ** -0.5)
        kh = conv_silu(H * Dh + h * Dh)
        vb = v_ref[:, h * Dh:(h + 1) * Dh]
        qb = qh.astype(BF16)
        kb = kh.astype(BF16)

        li = gt[h:h + 1, :]
        b_row = b_all[H + h:H + h + 1, :]
        g_tot = b_row[:, L - 1:L]
        rowterm = li - b_row
        w_state = g_tot + rowterm
        m_loc = jnp.max(w_state, axis=1, keepdims=True)
        e_col = to_col(jnp.exp(w_state - m_loc))
        b_col = to_col(b_row)

        c_prev = c_ref[h]
        n_prev = n_ref[h:h + 1, :]
        m_prev = m_ref[h:h + 1, 0:1]

        dm = jnp.where(tril, b_col + rowterm, NEG_BIG)
        m_intra = jnp.max(dm, axis=1, keepdims=True)
        m_inter = b_col + m_prev
        m_t = jnp.maximum(m_inter, m_intra)
        p = jnp.exp(dm - m_t)
        sqk = lax.dot_general(qb, kb, (((1,), (1,)), ((), ())), preferred_element_type=F32) * p
        inter = jnp.exp(m_inter - m_t)
        num = (jnp.dot(sqk.astype(BF16), vb, preferred_element_type=F32)
               + inter * jnp.dot(qb, c_prev.astype(BF16), preferred_element_type=F32))
        den = (jnp.sum(sqk, axis=1, keepdims=True)
               + inter * jnp.sum(qh * n_prev, axis=1, keepdims=True))
        hh = num / jnp.maximum(jnp.abs(den), jnp.exp(-m_t))
        og = o_ref[:, h * Dh:(h + 1) * Dh].astype(F32)
        out_ref[:, h * Dh:(h + 1) * Dh] = (_sigmoid(og) * hh).astype(BF16)

        ke = kh * e_col
        c_loc = lax.dot_general(ke.astype(BF16), vb, (((0,), (0,)), ((), ())),
                                preferred_element_type=F32)
        n_loc = jnp.sum(ke, axis=0, keepdims=True)
        m_new = jnp.maximum(g_tot + m_prev, m_loc)
        a = jnp.exp(g_tot + m_prev - m_new)
        bb = jnp.exp(m_loc - m_new)
        c_ref[h] = a * c_prev + bb * c_loc
        n_ref[h:h + 1, :] = a * n_prev + bb * n_loc
        m_ref[h:h + 1, :] = jnp.broadcast_to(m_new, (1, 128))

    ubuf[0:8, :] = ubuf[L:L + 8, :]


def _mlstm(P, gates, bif, conv_w, B, S):
    T = B * S
    L, H, Dh = M_CHUNK, M_HEADS, M_DH
    nc = S // L
    W = H * Dh
    blk = lambda cb: pl.BlockSpec((L, W), lambda b, c: (b * nc + c, cb))
    return pl.pallas_call(
        _mlstm_kernel,
        grid=(B, nc),
        in_specs=[blk(COL_MQ // W), blk(COL_MK // W), blk(COL_MV // W), blk(COL_MO // W),
                  pl.BlockSpec((GATE_ROWS, L), lambda b, c: (0, b * nc + c)),
                  pl.BlockSpec((GATE_ROWS, 1), lambda b, c: (0, 0)),
                  pl.BlockSpec((CONV_W, 2 * W), lambda b, c: (0, 0))],
        out_specs=pl.BlockSpec((L, W), lambda b, c: (b * nc + c, 0)),
        out_shape=jax.ShapeDtypeStruct((T, W), BF16),
        scratch_shapes=[pltpu.VMEM((L + 8, 2 * W), F32),
                        pltpu.VMEM((H, Dh, Dh), F32),
                        pltpu.VMEM((8, Dh), F32),
                        pltpu.VMEM((8, 128), F32)],
        compiler_params=_params(2),
        name="mlstm",
    )(P, P, P, P, gates, bif, conv_w)


def _qkprep_kernel(aq_ref, ak_ref, cos_ref, sin_ref, g_ref, gm_ref, q_out, k_out):
    tm = aq_ref.shape[0]
    cos = cos_ref[...]
    sin = sin_ref[...]
    lane = lax.broadcasted_iota(jnp.int32, (tm, 128), 1)
    first_half = (lane & (A_DK // 2)) == 0
    gm = gm_ref[...]
    for src, dst, gi, scale in ((aq_ref, q_out, 0, A_DK ** -0.5 * math.log2(math.e)),
                                (ak_ref, k_out, 1, 1.0)):
        g = g_ref[gi:gi + 1, :] * scale
        for c in range(A_HEADS):
            cols = slice(c * 128, (c + 1) * 128)
            x = src[:, cols].astype(F32)
            x2 = x * x
            hi = x2.astype(BF16)
            lo = (x2 - hi.astype(F32)).astype(BF16)
            ss = (jnp.dot(hi, gm, preferred_element_type=F32)
                  + jnp.dot(lo, gm, preferred_element_type=F32))
            y = x * lax.rsqrt(ss * (1.0 / A_DK) + NORM_EPS) * g
            partner = jnp.where(first_half, pltpu.roll(y, 128 - A_DK // 2, axis=1),
                                pltpu.roll(y, A_DK // 2, axis=1))
            dst[:, cols] = (y * cos + partner * sin).astype(BF16)


def _qkprep(P, cos_t, sin_t, g2, gmat, B, S):
    T = B * S
    tm = QKPREP_TM
    W = A_HEADS * 2 * A_DK
    spb = S // tm
    return pl.pallas_call(
        _qkprep_kernel,
        grid=(T // tm,),
        in_specs=[pl.BlockSpec((tm, W), lambda i: (i, COL_AQ // W)),
                  pl.BlockSpec((tm, W), lambda i: (i, COL_AK // W)),
                  pl.BlockSpec((tm, 128), lambda i: (i % spb, 0)),
                  pl.BlockSpec((tm, 128), lambda i: (i % spb, 0)),
                  pl.BlockSpec((2, 128), lambda i: (0, 0)),
                  pl.BlockSpec((128, 128), lambda i: (0, 0))],
        out_specs=[pl.BlockSpec((tm, W), lambda i: (i, 0)),
                   pl.BlockSpec((tm, W), lambda i: (i, 0))],
        out_shape=[jax.ShapeDtypeStruct((T, W), BF16), jax.ShapeDtypeStruct((T, W), BF16)],
        compiler_params=_params(1),
        name="qkprep",
    )(P, P, cos_t, sin_t, g2, gmat)


def _attn_kernel(*refs, lam_init, online):
    if online:
        q_ref, k_ref, v_ref, lam_ref, gd_ref, out_ref, qs_ref, vt_ref, l_ref, acc_ref, m_ref = refs
    else:
        c_ref, q_ref, k_ref, v_ref, lam_ref, gd_ref, out_ref, qs_ref, vt_ref, l_ref, acc_ref = refs
    t = q_ref.shape[0]
    nkv = k_ref.shape[0] // t
    qi = pl.program_id(2)

    @pl.when(qi == 0)
    def _():
        for c in range(nkv):
            vt_ref[c] = v_ref[c * t:(c + 1) * t, :].astype(F32).T.astype(BF16)

    q_t = q_ref[...].astype(F32).T
    sub = lax.broadcasted_iota(jnp.int32, (128, t), 0)
    qs_ref[:, 0:t] = jnp.where(sub < A_DK, q_t, 0.0).astype(BF16)
    qs_ref[:, t:2 * t] = jnp.where(sub >= A_DK, q_t, 0.0).astype(BF16)
    l_ref[...] = jnp.zeros(l_ref.shape, F32)
    acc_ref[...] = jnp.zeros(acc_ref.shape, F32)
    if online:
        m_ref[...] = jnp.full(m_ref.shape, NEG_BIG, F32)

    def step(j, masked):
        start = pl.multiple_of(j * t, t)
        k = k_ref[pl.ds(start, t), :]
        s = jnp.dot(k, qs_ref[...], preferred_element_type=F32)
        if masked:
            row = lax.broadcasted_iota(jnp.int32, (t, 2 * t), 0)
            col = lax.broadcasted_iota(jnp.int32, (t, 2 * t), 1)
            qcol = jnp.where(col >= t, col - t, col)
            s = jnp.where(row <= qcol, s, NEG_BIG)
        if online:
            m_prev = m_ref[...]
            m_new = jnp.maximum(m_prev, jnp.max(s, axis=0, keepdims=True))
            alpha = jnp.exp2(m_prev - m_new)
            p = jnp.exp2(s - m_new)
            l_ref[...] = alpha * l_ref[...] + jnp.sum(p, axis=0, keepdims=True)
            acc_ref[...] = alpha * acc_ref[...] + jnp.dot(vt_ref[j], p.astype(BF16),
                                                          preferred_element_type=F32)
            m_ref[...] = m_new
        else:
            p = jnp.exp2(s - c_ref[0])
            l_ref[...] += jnp.sum(p, axis=0, keepdims=True)
            acc_ref[...] += jnp.dot(vt_ref[j], p.astype(BF16), preferred_element_type=F32)

    def body(j, carry):
        step(j, False)
        return carry

    lax.fori_loop(0, qi, body, 0)
    step(qi, True)

    accn = acc_ref[...] / l_ref[...]
    lp = lam_ref[...]
    lam = (jnp.exp(jnp.sum(lp[0:1, :] * lp[1:2, :], axis=1, keepdims=True))
           - jnp.exp(jnp.sum(lp[2:3, :] * lp[3:4, :], axis=1, keepdims=True)) + lam_init)
    o_t = accn[:, 0:t] - lam * accn[:, t:2 * t]
    ms = jnp.mean(o_t * o_t, axis=0, keepdims=True)
    o = (o_t * lax.rsqrt(ms + NORM_EPS)).T
    out_ref[...] = (o * (gd_ref[...] * (1.0 - lam_init))).astype(BF16)


def _attn(c, qr, kr, P, lam_p, gd, *, lam_init, B, S, online):
    T = B * S
    t = ATTN_T
    nq = S // t
    in_specs = [pl.BlockSpec((t, 128), lambda b, h, i: (b * nq + i, h)),
                pl.BlockSpec((S, 128), lambda b, h, i: (b, h)),
                pl.BlockSpec((S, 128), lambda b, h, i: (b, COL_AV // 128 + h)),
                pl.BlockSpec((4, A_DK), lambda b, h, i: (0, 0)),
                pl.BlockSpec((1, A_DV), lambda b, h, i: (0, 0))]
    scratch = [pltpu.VMEM((128, 2 * t), BF16),
               pltpu.VMEM((nq, 128, t), BF16),
               pltpu.VMEM((1, 2 * t), F32),
               pltpu.VMEM((128, 2 * t), F32)]
    args = (qr, kr, P, lam_p, gd)
    if online:
        scratch = scratch + [pltpu.VMEM((1, 2 * t), F32)]
    else:
        in_specs = [pl.BlockSpec(memory_space=pltpu.SMEM)] + in_specs
        args = (c,) + args
    return pl.pallas_call(
        functools.partial(_attn_kernel, lam_init=lam_init, online=online),
        grid=(B, A_HEADS, nq),
        in_specs=in_specs,
        out_specs=pl.BlockSpec((t, 128), lambda b, h, i: (b * nq + i, h)),
        out_shape=jax.ShapeDtypeStruct((T, A_HEADS * A_DV), BF16),
        scratch_shapes=scratch,
        compiler_params=_params(3),
        name="attn_online" if online else "attn",
    )(*args)


def _merge_kernel(hm_ref, pu_ref, ao_ref, gm_ref, gp_ref, ga_ref, x_ref, wm_ref, wp_ref, ps_ref,
                  wd_ref, wo_ref, out_ref, pbuf, *, tiles_per_seq):
    tm = x_ref.shape[0]
    G = POOL_G
    i = pl.program_id(0)
    tile_in_seq = i % tiles_per_seq

    @pl.when(tile_in_seq == 0)
    def _():
        pbuf[0:POOL_HALO, :] = jnp.zeros((POOL_HALO, D_MODEL), F32)

    pbuf[POOL_HALO:POOL_HALO + tm, :] = pu_ref[...].astype(F32)
    pos1 = tile_in_seq * tm + lax.broadcasted_iota(jnp.int32, (tm, 1), 0) + 1

    yp = []
    for gi, w in enumerate(POOL_WINDOWS):
        cols = slice(gi * G, (gi + 1) * G)
        u = pbuf[POOL_HALO:POOL_HALO + tm, cols]
        acc = u
        for j in range(1, w):
            acc = acc + pbuf[POOL_HALO - j:POOL_HALO - j + tm, cols]
        cnt = jnp.minimum(pos1, w).astype(F32)
        pooled = acc / cnt - u
        yp.append(jnp.dot(pooled.astype(BF16), wp_ref[gi], preferred_element_type=F32))
    pbuf[0:POOL_HALO, :] = pbuf[tm:tm + POOL_HALO, :]

    y_p = jnp.concatenate(yp, axis=1) * ps_ref[...]
    y_m = jnp.dot(hm_ref[...], wm_ref[...], preferred_element_type=F32)
    y_a = jnp.dot(ao_ref[...], wd_ref[...], preferred_element_type=F32)
    g_m = _sigmoid(gm_ref[...].astype(F32))
    g_p = _sigmoid(gp_ref[...].astype(F32))
    g_a = _sigmoid(ga_ref[...].astype(F32))
    merged = (g_m * y_m + g_p * y_p + g_a * y_a).astype(BF16)
    out_ref[...] = x_ref[...] + jnp.dot(merged, wo_ref[...], preferred_element_type=F32)


def _merge(hm, P, ao, xf, wm, wp, ps, wd, wo, B, S):
    T = B * S
    tm = MERGE_TM
    row = lambda w, cb: pl.BlockSpec((tm, w), lambda i: (i, cb))
    full = lambda shape: pl.BlockSpec(shape, lambda i: (0,) * len(shape))
    return pl.pallas_call(
        functools.partial(_merge_kernel, tiles_per_seq=S // tm),
        grid=(T // tm,),
        in_specs=[row(D_MODEL, 0), row(D_MODEL, COL_PU // D_MODEL), row(D_MODEL, 0),
                  row(D_MODEL, COL_GATE // D_MODEL), row(D_MODEL, COL_GATE // D_MODEL + 1),
                  row(D_MODEL, COL_GATE // D_MODEL + 2), row(D_MODEL, 0),
                  full((D_MODEL, D_MODEL)), full((4, POOL_G, POOL_G)), full((1, D_MODEL)),
                  full((D_MODEL, D_MODEL)), full((D_MODEL, D_MODEL))],
        out_specs=row(D_MODEL, 0),
        out_shape=jax.ShapeDtypeStruct((T, D_MODEL), F32),
        scratch_shapes=[pltpu.VMEM((tm + POOL_HALO, D_MODEL), F32)],
        compiler_params=_params(1),
        name="merge",
    )(hm, P, ao, P, P, P, xf, wm, wp, ps, wd, wo)


def _ffn_kernel(x_ref, g_ref, wgu_ref, wdn_ref, out_ref, act_ref):
    x = x_ref[...]
    ms = jnp.mean(x * x, axis=-1, keepdims=True)
    hb = (x * lax.rsqrt(ms + NORM_EPS) * g_ref[...]).astype(BF16)
    for lo, hi in FFN_CHUNKS:
        gate = jnp.dot(hb, wgu_ref[:, lo:hi], preferred_element_type=F32)
        up = jnp.dot(hb, wgu_ref[:, FF + lo:FF + hi], preferred_element_type=F32)
        act_ref[:, lo:hi] = (gate * _sigmoid(gate) * up).astype(BF16)
    out_ref[...] = x + jnp.dot(act_ref[...], wdn_ref[...], preferred_element_type=F32)


def _ffn(xf, g, wgu, wdn):
    T = xf.shape[0]
    tm = FFN_TM
    const = lambda shape: pl.BlockSpec(shape, lambda i: (0, 0), pipeline_mode=pl.Buffered(1))
    return pl.pallas_call(
        _ffn_kernel,
        grid=(T // tm,),
        in_specs=[pl.BlockSpec((tm, D_MODEL), lambda i: (i, 0)),
                  pl.BlockSpec((1, D_MODEL), lambda i: (0, 0)),
                  const((D_MODEL, 2 * FF)), const((FF, D_MODEL))],
        out_specs=pl.BlockSpec((tm, D_MODEL), lambda i: (i, 0)),
        out_shape=jax.ShapeDtypeStruct((T, D_MODEL), F32),
        scratch_shapes=[pltpu.VMEM((tm, FF), BF16)],
        compiler_params=_params(1),
        name="ffn",
    )(xf, g, wgu, wdn)


def _rope_tables(S):
    half = A_DK // 2
    inv = ROPE_THETA ** (-jnp.arange(half, dtype=F32) / half)
    ang = jnp.arange(S, dtype=F32)[:, None] * inv[None, :]
    cos, sin = jnp.cos(ang), jnp.sin(ang)
    cos_t = jnp.concatenate([cos, cos, cos, cos], axis=1)
    sin_t = jnp.concatenate([-sin, sin, -sin, sin], axis=1)
    return cos_t, sin_t


def kernel(x, g_mix, w_in, b_if, conv_qk, w_m_out, w_pool, pool_scale, g_qk, lam_p, g_diff_head,
           w_diff_out, w_out, g_ffn, w_gate_up, w_down):
    B, S, _ = x.shape
    T = B * S
    xf = x.reshape(T, D_MODEL)
    cos_t, sin_t = _rope_tables(S)
    lane = jnp.arange(128)
    gmat = (lane[:, None] // A_DK == lane[None, :] // A_DK).astype(BF16)
    n_gate = 2 * M_HEADS
    gate_lo = 4 * M_HEADS * M_DH

    for l in range(DEPTH):
        wl = w_in[l]
        w_main = jnp.concatenate([wl[:, :gate_lo], wl[:, gate_lo + n_gate:]], axis=1).astype(BF16)
        wif_t = jnp.pad(wl[:, gate_lo:gate_lo + n_gate].T, ((0, GATE_ROWS - n_gate), (0, 0))).astype(BF16)
        bif = jnp.pad(b_if[l], (0, GATE_ROWS - n_gate)).reshape(GATE_ROWS, 1)
        g2 = jnp.concatenate([g_qk[l], g_qk[l]], axis=1)
        lam_init = 0.8 - 0.6 * math.exp(-0.3 * l)

        P, gates = _proj(xf, g_mix[l].reshape(1, D_MODEL), w_main, wif_t)
        hm = _mlstm(P, gates, bif, conv_qk[l], B, S)
        qr, kr = _qkprep(P, cos_t, sin_t, g2, gmat, B, S)
        c = (A_DK ** 0.5 * math.log2(math.e)) * jnp.max(jnp.abs(g_qk[l][0])) * jnp.max(jnp.abs(g_qk[l][1]))
        attn_args = (c.reshape(1), qr, kr, P, lam_p[l], g_diff_head[l].reshape(1, A_DV))
        ao = lax.cond(c < ATTN_MAX_STATIC_BOUND,
                      functools.partial(_attn, lam_init=lam_init, B=B, S=S, online=False),
                      functools.partial(_attn, lam_init=lam_init, B=B, S=S, online=True),
                      *attn_args)
        x1 = _merge(hm, P, ao, xf, w_m_out[l].astype(BF16), w_pool[l].astype(BF16),
                    pool_scale[l].reshape(1, D_MODEL), w_diff_out[l].astype(BF16),
                    w_out[l].astype(BF16), B, S)
        xf = _ffn(x1, g_ffn[l].reshape(1, D_MODEL), w_gate_up[l].astype(BF16),
                  w_down[l].astype(BF16))
    return xf.reshape(B, S, D_MODEL)
```

```python
import functools
import math

import jax
import jax.numpy as jnp
from jax import lax
from jax.experimental import pallas as pl
from jax.experimental.pallas import tpu as pltpu

D_MODEL = 1024
DEPTH = 2
M_HEADS = 4
M_DH = 256
M_CHUNK = 128
CONV_W = 4
POOL_WINDOWS = (2, 4, 8, 16)
POOL_G = 256
POOL_HALO = 16
A_HEADS = 8
A_DK = 64
A_DV = 128
FF = 2816
NORM_EPS = 1e-6
NEG_BIG = -1e30
ROPE_THETA = 10000.0

COL_MQ, COL_MK, COL_MV, COL_MO = 0, 1024, 2048, 3072
COL_PU, COL_AQ, COL_AK, COL_AV, COL_GATE = 4096, 5120, 6144, 7168, 8192
N_MAIN = 11264
GATE_ROWS = 16

BF16 = jnp.bfloat16
F32 = jnp.float32

VMEM_LIMIT = 56 * 1024 * 1024

PROJ_TM, PROJ_TN = 1024, 1024
QKPREP_TM = 512
ATTN_T = 512
ATTN_MAX_STATIC_BOUND = 60.0
MERGE_TM = 512
POOL_BAND_ROWS = 256
FFN_TM = 256
FFN_CHUNKS = ((0, 512), (512, 1024), (1024, 1536), (1536, 2048), (2048, 2560), (2560, 2816))


def _params(n_axes):
    return pltpu.CompilerParams(dimension_semantics=("arbitrary",) * n_axes,
                                vmem_limit_bytes=VMEM_LIMIT)


def _sigmoid(x):
    return 1.0 / (1.0 + jnp.exp(-x))


def _proj_kernel(x_ref, g_ref, w_ref, wif_ref, out_ref, gates_ref, h_ref):
    @pl.when(pl.program_id(1) == 0)
    def _():
        x = x_ref[...]
        ms = jnp.mean(x * x, axis=-1, keepdims=True)
        hb = (x * lax.rsqrt(ms + NORM_EPS) * g_ref[...]).astype(BF16)
        h_ref[...] = hb
        gates_ref[...] = lax.dot_general(wif_ref[...], hb, (((1,), (1,)), ((), ())),
                                         preferred_element_type=F32)

    out_ref[...] = jnp.dot(h_ref[...], w_ref[...], preferred_element_type=F32).astype(BF16)


def _proj(xf, g, w_main, wif_t):
    T = xf.shape[0]
    tm, tn = PROJ_TM, PROJ_TN
    return pl.pallas_call(
        _proj_kernel,
        grid=(T // tm, N_MAIN // tn),
        in_specs=[pl.BlockSpec((tm, D_MODEL), lambda i, j: (i, 0)),
                  pl.BlockSpec((1, D_MODEL), lambda i, j: (0, 0)),
                  pl.BlockSpec((D_MODEL, tn), lambda i, j: (0, j)),
                  pl.BlockSpec((GATE_ROWS, D_MODEL), lambda i, j: (0, 0))],
        out_specs=[pl.BlockSpec((tm, tn), lambda i, j: (i, j)),
                   pl.BlockSpec((GATE_ROWS, tm), lambda i, j: (0, i))],
        out_shape=[jax.ShapeDtypeStruct((T, N_MAIN), BF16),
                   jax.ShapeDtypeStruct((GATE_ROWS, T), F32)],
        scratch_shapes=[pltpu.VMEM((tm, D_MODEL), BF16)],
        compiler_params=_params(2),
        name="proj",
    )(xf, g, w_main, wif_t)


def _mlstm_kernel(q_ref, k_ref, v_ref, o_ref, gt_ref, bif_ref, cw_ref, out_ref,
                  ubuf, c_ref, n_ref, m_ref):
    L, H, Dh = M_CHUNK, M_HEADS, M_DH

    @pl.when(pl.program_id(1) == 0)
    def _():
        ubuf[0:8, :] = jnp.zeros((8, 2 * H * Dh), F32)
        c_ref[...] = jnp.zeros(c_ref.shape, F32)
        n_ref[...] = jnp.zeros(n_ref.shape, F32)
        m_ref[...] = jnp.full(m_ref.shape, NEG_BIG, F32)

    ubuf[8:8 + L, 0:H * Dh] = q_ref[...].astype(F32)
    ubuf[8:8 + L, H * Dh:2 * H * Dh] = k_ref[...].astype(F32)

    gt = gt_ref[...] + bif_ref[...]
    lf = jnp.minimum(gt, 0.0) - jnp.log(1.0 + jnp.exp(-jnp.abs(gt)))
    lane = lax.broadcasted_iota(jnp.int32, (GATE_ROWS, L), 1)
    b_all = lf
    for sh in (1, 2, 4, 8, 16, 32, 64):
        b_all = b_all + jnp.where(lane >= sh, pltpu.roll(b_all, sh, axis=1), 0.0)

    row = lax.broadcasted_iota(jnp.int32, (L, L), 0)
    col = lax.broadcasted_iota(jnp.int32, (L, L), 1)
    eye = row == col
    tril = row >= col

    def to_col(r):
        return jnp.sum(jnp.where(eye, jnp.broadcast_to(r, (L, L)), 0.0), axis=1, keepdims=True)

    def conv_silu(c0):
        cols = slice(c0, c0 + Dh)
        y = (cw_ref[3:4, cols] * ubuf[8:8 + L, cols] + cw_ref[2:3, cols] * ubuf[7:7 + L, cols]
             + cw_ref[1:2, cols] * ubuf[6:6 + L, cols] + cw_ref[0:1, cols] * ubuf[5:5 + L, cols])
        return y * _sigmoid(y)

    for h in range(H):
        qh = conv_silu(h * Dh) * (Dh ** -0.5)
        kh = conv_silu(H * Dh + h * Dh)
        vb = v_ref[:, h * Dh:(h + 1) * Dh]
        qb = qh.astype(BF16)
        kb = kh.astype(BF16)

        li = gt[h:h + 1, :]
        b_row = b_all[H + h:H + h + 1, :]
        g_tot = b_row[:, L - 1:L]
        rowterm = li - b_row
        w_state = g_tot + rowterm
        m_loc = jnp.max(w_state, axis=1, keepdims=True)
        e_col = to_col(jnp.exp(w_state - m_loc))
        b_col = to_col(b_row)

        c_prev = c_ref[h]
        n_prev = n_ref[h:h + 1, :]
        m_prev = m_ref[h:h + 1, 0:1]

        dm = jnp.where(tril, b_col + rowterm, NEG_BIG)
        m_intra = jnp.max(dm, axis=1, keepdims=True)
        m_inter = b_col + m_prev
        m_t = jnp.maximum(m_inter, m_intra)
        p = jnp.exp(dm - m_t)
        sqk = lax.dot_general(qb, kb, (((1,), (1,)), ((), ())), preferred_element_type=F32) * p
        inter = jnp.exp(m_inter - m_t)
        num = (jnp.dot(sqk.astype(BF16), vb, preferred_element_type=F32)
               + inter * jnp.dot(qb, c_prev.astype(BF16), preferred_element_type=F32))
        den = (jnp.sum(sqk, axis=1, keepdims=True)
               + inter * jnp.sum(qh * n_prev, axis=1, keepdims=True))
        hh = num / jnp.maximum(jnp.abs(den), jnp.exp(-m_t))
        og = o_ref[:, h * Dh:(h + 1) * Dh].astype(F32)
        out_ref[:, h * Dh:(h + 1) * Dh] = (_sigmoid(og) * hh).astype(BF16)

        ke = kh * e_col
        c_loc = lax.dot_general(ke.astype(BF16), vb, (((0,), (0,)), ((), ())),
                                preferred_element_type=F32)
        n_loc = jnp.sum(ke, axis=0, keepdims=True)
        m_new = jnp.maximum(g_tot + m_prev, m_loc)
        a = jnp.exp(g_tot + m_prev - m_new)
        bb = jnp.exp(m_loc - m_new)
        c_ref[h] = a * c_prev + bb * c_loc
        n_ref[h:h + 1, :] = a * n_prev + bb * n_loc
        m_ref[h:h + 1, :] = jnp.broadcast_to(m_new, (1, 128))

    ubuf[0:8, :] = ubuf[L:L + 8, :]


def _mlstm(P, gates, bif, conv_w, B, S):
    T = B * S
    L, H, Dh = M_CHUNK, M_HEADS, M_DH
    nc = S // L
    W = H * Dh
    blk = lambda cb: pl.BlockSpec((L, W), lambda b, c: (b * nc + c, cb))
    return pl.pallas_call(
        _mlstm_kernel,
        grid=(B, nc),
        in_specs=[blk(COL_MQ // W), blk(COL_MK // W), blk(COL_MV // W), blk(COL_MO // W),
                  pl.BlockSpec((GATE_ROWS, L), lambda b, c: (0, b * nc + c)),
                  pl.BlockSpec((GATE_ROWS, 1), lambda b, c: (0, 0)),
                  pl.BlockSpec((CONV_W, 2 * W), lambda b, c: (0, 0))],
        out_specs=pl.BlockSpec((L, W), lambda b, c: (b * nc + c, 0)),
        out_shape=jax.ShapeDtypeStruct((T, W), BF16),
        scratch_shapes=[pltpu.VMEM((L + 8, 2 * W), F32),
                        pltpu.VMEM((H, Dh, Dh), F32),
                        pltpu.VMEM((8, Dh), F32),
                        pltpu.VMEM((8, 128), F32)],
        compiler_params=_params(2),
        name="mlstm",
    )(P, P, P, P, gates, bif, conv_w)


def _qkprep_kernel(aq_ref, ak_ref, cos_ref, sin_ref, g_ref, gm_ref, q_out, k_out):
    tm = aq_ref.shape[0]
    cos = cos_ref[...]
    sin = sin_ref[...]
    lane = lax.broadcasted_iota(jnp.int32, (tm, 128), 1)
    first_half = (lane & (A_DK // 2)) == 0
    gm = gm_ref[...]
    for src, dst, gi, scale in ((aq_ref, q_out, 0, A_DK ** -0.5 * math.log2(math.e)),
                                (ak_ref, k_out, 1, 1.0)):
        g = g_ref[gi:gi + 1, :] * scale
        for c in range(A_HEADS):
            cols = slice(c * 128, (c + 1) * 128)
            x = src[:, cols].astype(F32)
            x2 = x * x
            ss = jnp.dot(x2.astype(BF16), gm, preferred_element_type=F32)
            y = x * lax.rsqrt(ss * (1.0 / A_DK) + NORM_EPS) * g
            partner = jnp.where(first_half, pltpu.roll(y, 128 - A_DK // 2, axis=1),
                                pltpu.roll(y, A_DK // 2, axis=1))
            dst[:, cols] = (y * cos + partner * sin).astype(BF16)


def _qkprep(P, cos_t, sin_t, g2, gmat, B, S):
    T = B * S
    tm = QKPREP_TM
    W = A_HEADS * 2 * A_DK
    spb = S // tm
    return pl.pallas_call(
        _qkprep_kernel,
        grid=(T // tm,),
        in_specs=[pl.BlockSpec((tm, W), lambda i: (i, COL_AQ // W)),
                  pl.BlockSpec((tm, W), lambda i: (i, COL_AK // W)),
                  pl.BlockSpec((tm, 128), lambda i: (i % spb, 0)),
                  pl.BlockSpec((tm, 128), lambda i: (i % spb, 0)),
                  pl.BlockSpec((2, 128), lambda i: (0, 0)),
                  pl.BlockSpec((128, 128), lambda i: (0, 0))],
        out_specs=[pl.BlockSpec((tm, W), lambda i: (i, 0)),
                   pl.BlockSpec((tm, W), lambda i: (i, 0))],
        out_shape=[jax.ShapeDtypeStruct((T, W), BF16), jax.ShapeDtypeStruct((T, W), BF16)],
        compiler_params=_params(1),
        name="qkprep",
    )(P, P, cos_t, sin_t, g2, gmat)


def _attn_kernel(*refs, lam_init, online):
    if online:
        q_ref, k_ref, v_ref, lam_ref, gd_ref, out_ref, qs_ref, vt_ref, l_ref, acc_ref, m_ref = refs
    else:
        c_ref, q_ref, k_ref, v_ref, lam_ref, gd_ref, out_ref, qs_ref, vt_ref, l_ref, acc_ref = refs
    t = q_ref.shape[0]
    nkv = k_ref.shape[0] // t
    qi = pl.program_id(2)

    @pl.when(qi == 0)
    def _():
        for c in range(nkv):
            vt_ref[:, c * t:(c + 1) * t] = v_ref[c * t:(c + 1) * t, :].astype(F32).T.astype(BF16)

    q_t = q_ref[...].astype(F32).T
    sub = lax.broadcasted_iota(jnp.int32, (128, t), 0)
    qs_ref[:, 0:t] = jnp.where(sub < A_DK, q_t, 0.0).astype(BF16)
    qs_ref[:, t:2 * t] = jnp.where(sub >= A_DK, q_t, 0.0).astype(BF16)
    l_ref[...] = jnp.zeros(l_ref.shape, F32)
    acc_ref[...] = jnp.zeros(acc_ref.shape, F32)
    if online:
        m_ref[...] = jnp.full(m_ref.shape, NEG_BIG, F32)

    def step(j, nb, diag_last):
        start = pl.multiple_of(j * t, t)
        k = k_ref[pl.ds(start, nb * t), :]
        vt = vt_ref[:, pl.ds(start, nb * t)]
        s = jnp.dot(k, qs_ref[...], preferred_element_type=F32)
        if diag_last:
            row = lax.broadcasted_iota(jnp.int32, (nb * t, 2 * t), 0) - (nb - 1) * t
            col = lax.broadcasted_iota(jnp.int32, (nb * t, 2 * t), 1)
            qcol = jnp.where(col >= t, col - t, col)
            s = jnp.where(row <= qcol, s, NEG_BIG)
        if online:
            m_prev = m_ref[...]
            m_new = jnp.maximum(m_prev, jnp.max(s, axis=0, keepdims=True))
            alpha = jnp.exp2(m_prev - m_new)
            p = jnp.exp2(s - m_new)
            l_ref[...] = alpha * l_ref[...] + jnp.sum(p, axis=0, keepdims=True)
            acc_ref[...] = alpha * acc_ref[...] + jnp.dot(vt, p.astype(BF16),
                                                          preferred_element_type=F32)
            m_ref[...] = m_new
        else:
            p = jnp.exp2(s - c_ref[0])
            l_ref[...] += jnp.sum(p, axis=0, keepdims=True)
            acc_ref[...] += jnp.dot(vt, p.astype(BF16), preferred_element_type=F32)

    def body(jj, carry):
        step(2 * jj, 2, False)
        return carry

    lax.fori_loop(0, qi // 2, body, 0)

    @pl.when(qi % 2 == 1)
    def _():
        step(qi - 1, 2, True)

    @pl.when(qi % 2 == 0)
    def _():
        step(qi, 1, True)

    accn = acc_ref[...] / l_ref[...]
    lp = lam_ref[...]
    lam = (jnp.exp(jnp.sum(lp[0:1, :] * lp[1:2, :], axis=1, keepdims=True))
           - jnp.exp(jnp.sum(lp[2:3, :] * lp[3:4, :], axis=1, keepdims=True)) + lam_init)
    o_t = accn[:, 0:t] - lam * accn[:, t:2 * t]
    ms = jnp.mean(o_t * o_t, axis=0, keepdims=True)
    o = (o_t * lax.rsqrt(ms + NORM_EPS)).T
    out_ref[...] = (o * (gd_ref[...] * (1.0 - lam_init))).astype(BF16)


def _attn(c, qr, kr, P, lam_p, gd, *, lam_init, B, S, online):
    T = B * S
    t = ATTN_T
    nq = S // t
    in_specs = [pl.BlockSpec((t, 128), lambda b, h, i: (b * nq + i, h)),
                pl.BlockSpec((S, 128), lambda b, h, i: (b, h)),
                pl.BlockSpec((S, 128), lambda b, h, i: (b, COL_AV // 128 + h)),
                pl.BlockSpec((4, A_DK), lambda b, h, i: (0, 0)),
                pl.BlockSpec((1, A_DV), lambda b, h, i: (0, 0))]
    scratch = [pltpu.VMEM((128, 2 * t), BF16),
               pltpu.VMEM((128, S), BF16),
               pltpu.VMEM((1, 2 * t), F32),
               pltpu.VMEM((128, 2 * t), F32)]
    args = (qr, kr, P, lam_p, gd)
    if online:
        scratch = scratch + [pltpu.VMEM((1, 2 * t), F32)]
    else:
        in_specs = [pl.BlockSpec(memory_space=pltpu.SMEM)] + in_specs
        args = (c,) + args
    return pl.pallas_call(
        functools.partial(_attn_kernel, lam_init=lam_init, online=online),
        grid=(B, A_HEADS, nq),
        in_specs=in_specs,
        out_specs=pl.BlockSpec((t, 128), lambda b, h, i: (b * nq + i, h)),
        out_shape=jax.ShapeDtypeStruct((T, A_HEADS * A_DV), BF16),
        scratch_shapes=scratch,
        compiler_params=_params(3),
        name="attn_online" if online else "attn",
    )(*args)


def _merge_kernel(hm_ref, pu_ref, ao_ref, gm_ref, gp_ref, ga_ref, x_ref, wm_ref, wp_ref, ps_ref,
                  wd_ref, wo_ref, bm_ref, bh_ref, out_ref, halo_ref, mrg_ref, win_ref,
                  *, tiles_per_seq):
    tm = x_ref.shape[0]
    G = POOL_G
    i = pl.program_id(0)
    tile_in_seq = i % tiles_per_seq

    @pl.when(tile_in_seq == 0)
    def _():
        halo_ref[...] = jnp.zeros(halo_ref.shape, BF16)

    pos1 = tile_in_seq * tm + lax.broadcasted_iota(jnp.int32, (tm, 1), 0) + 1

    y_m = jnp.dot(hm_ref[...], wm_ref[...], preferred_element_type=F32)
    y_a = jnp.dot(ao_ref[...], wd_ref[...], preferred_element_type=F32)
    mrg_ref[...] = (_sigmoid(gm_ref[...].astype(F32)) * y_m
                    + _sigmoid(ga_ref[...].astype(F32)) * y_a)

    tb = bm_ref.shape[1]
    for r in range(tm // tb):
        rows = slice(r * tb, (r + 1) * tb)
        for gi, w in enumerate(POOL_WINDOWS):
            cols = slice(gi * G, (gi + 1) * G)
            ub = pu_ref[rows, cols]
            prev = halo_ref[:, cols] if r == 0 else pu_ref[r * tb - POOL_HALO:r * tb, cols]
            win_ref[...] = jnp.dot(bm_ref[gi], ub, preferred_element_type=F32)
            win_ref[0:POOL_HALO, :] += jnp.dot(bh_ref[gi], prev, preferred_element_type=F32)
            cnt = jnp.minimum(pos1[rows], w).astype(F32)
            pooled = win_ref[...] / cnt - ub.astype(F32)
            y_p = (jnp.dot(pooled.astype(BF16), wp_ref[gi], preferred_element_type=F32)
                   * ps_ref[:, cols])
            mrg_ref[rows, cols] += _sigmoid(gp_ref[rows, cols].astype(F32)) * y_p
    halo_ref[...] = pu_ref[tm - POOL_HALO:tm, :]

    out_ref[...] = x_ref[...] + jnp.dot(mrg_ref[...].astype(BF16), wo_ref[...],
                                        preferred_element_type=F32)


def _pool_bands(tm):
    t_idx = jnp.arange(tm)[:, None]
    main, halo = [], []
    for w in POOL_WINDOWS:
        d_main = t_idx - jnp.arange(tm)[None, :]
        d_halo = t_idx[:POOL_HALO] - (jnp.arange(POOL_HALO)[None, :] - POOL_HALO)
        main.append((d_main >= 0) & (d_main < w))
        halo.append((d_halo >= 0) & (d_halo < w))
    return jnp.stack(main).astype(BF16), jnp.stack(halo).astype(BF16)


def _merge(hm, P, ao, xf, wm, wp, ps, wd, wo, B, S):
    T = B * S
    tm = MERGE_TM
    tb = POOL_BAND_ROWS
    band_main, band_halo = _pool_bands(tb)
    row = lambda w, cb: pl.BlockSpec((tm, w), lambda i: (i, cb))
    full = lambda shape: pl.BlockSpec(shape, lambda i: (0,) * len(shape),
                                      pipeline_mode=pl.Buffered(1))
    return pl.pallas_call(
        functools.partial(_merge_kernel, tiles_per_seq=S // tm),
        grid=(T // tm,),
        in_specs=[row(D_MODEL, 0), row(D_MODEL, COL_PU // D_MODEL), row(D_MODEL, 0),
                  row(D_MODEL, COL_GATE // D_MODEL), row(D_MODEL, COL_GATE // D_MODEL + 1),
                  row(D_MODEL, COL_GATE // D_MODEL + 2), row(D_MODEL, 0),
                  full((D_MODEL, D_MODEL)), full((4, POOL_G, POOL_G)), full((1, D_MODEL)),
                  full((D_MODEL, D_MODEL)), full((D_MODEL, D_MODEL)),
                  full((4, tb, tb)), full((4, POOL_HALO, POOL_HALO))],
        out_specs=row(D_MODEL, 0),
        out_shape=jax.ShapeDtypeStruct((T, D_MODEL), F32),
        scratch_shapes=[pltpu.VMEM((POOL_HALO, D_MODEL), BF16),
                        pltpu.VMEM((tm, D_MODEL), F32),
                        pltpu.VMEM((tb, POOL_G), F32)],
        compiler_params=_params(1),
        name="merge",
    )(hm, P, ao, P, P, P, xf, wm, wp, ps, wd, wo, band_main, band_halo)


def _ffn_kernel(x_ref, g_ref, wgu_ref, wdn_ref, out_ref, act_ref):
    x = x_ref[...]
    ms = jnp.mean(x * x, axis=-1, keepdims=True)
    hb = (x * lax.rsqrt(ms + NORM_EPS) * g_ref[...]).astype(BF16)
    for lo, hi in FFN_CHUNKS:
        gate = jnp.dot(hb, wgu_ref[:, lo:hi], preferred_element_type=F32)
        up = jnp.dot(hb, wgu_ref[:, FF + lo:FF + hi], preferred_element_type=F32)
        act_ref[:, lo:hi] = (gate * _sigmoid(gate) * up).astype(BF16)
    out_ref[...] = x + jnp.dot(act_ref[...], wdn_ref[...], preferred_element_type=F32)


def _ffn(xf, g, wgu, wdn):
    T = xf.shape[0]
    tm = FFN_TM
    const = lambda shape: pl.BlockSpec(shape, lambda i: (0, 0), pipeline_mode=pl.Buffered(1))
    return pl.pallas_call(
        _ffn_kernel,
        grid=(T // tm,),
        in_specs=[pl.BlockSpec((tm, D_MODEL), lambda i: (i, 0)),
                  pl.BlockSpec((1, D_MODEL), lambda i: (0, 0)),
                  const((D_MODEL, 2 * FF)), const((FF, D_MODEL))],
        out_specs=pl.BlockSpec((tm, D_MODEL), lambda i: (i, 0)),
        out_shape=jax.ShapeDtypeStruct((T, D_MODEL), F32),
        scratch_shapes=[pltpu.VMEM((tm, FF), BF16)],
        compiler_params=_params(1),
        name="ffn",
    )(xf, g, wgu, wdn)


def _rope_tables(S):
    half = A_DK // 2
    inv = ROPE_THETA ** (-jnp.arange(half, dtype=F32) / half)
    ang = jnp.arange(S, dtype=F32)[:, None] * inv[None, :]
    cos, sin = jnp.cos(ang), jnp.sin(ang)
    cos_t = jnp.concatenate([cos, cos, cos, cos], axis=1)
    sin_t = jnp.concatenate([-sin, sin, -sin, sin], axis=1)
    return cos_t, sin_t


def kernel(x, g_mix, w_in, b_if, conv_qk, w_m_out, w_pool, pool_scale, g_qk, lam_p, g_diff_head,
           w_diff_out, w_out, g_ffn, w_gate_up, w_down):
    B, S, _ = x.shape
    T = B * S
    xf = x.reshape(T, D_MODEL)
    cos_t, sin_t = _rope_tables(S)
    lane = jnp.arange(128)
    gmat = (lane[:, None] // A_DK == lane[None, :] // A_DK).astype(BF16)
    n_gate = 2 * M_HEADS
    gate_lo = 4 * M_HEADS * M_DH

    for l in range(DEPTH):
        wl = w_in[l]
        w_main = jnp.concatenate([wl[:, :gate_lo], wl[:, gate_lo + n_gate:]], axis=1).astype(BF16)
        wif_t = jnp.pad(wl[:, gate_lo:gate_lo + n_gate].T, ((0, GATE_ROWS - n_gate), (0, 0))).astype(BF16)
        bif = jnp.pad(b_if[l], (0, GATE_ROWS - n_gate)).reshape(GATE_ROWS, 1)
        g2 = jnp.concatenate([g_qk[l], g_qk[l]], axis=1)
        lam_init = 0.8 - 0.6 * math.exp(-0.3 * l)

        P, gates = _proj(xf, g_mix[l].reshape(1, D_MODEL), w_main, wif_t)
        hm = _mlstm(P, gates, bif, conv_qk[l], B, S)
        qr, kr = _qkprep(P, cos_t, sin_t, g2, gmat, B, S)
        c = (A_DK ** 0.5 * math.log2(math.e)) * jnp.max(jnp.abs(g_qk[l][0])) * jnp.max(jnp.abs(g_qk[l][1]))
        attn_args = (c.reshape(1), qr, kr, P, lam_p[l], g_diff_head[l].reshape(1, A_DV))
        ao = lax.cond(c < ATTN_MAX_STATIC_BOUND,
                      functools.partial(_attn, lam_init=lam_init, B=B, S=S, online=False),
                      functools.partial(_attn, lam_init=lam_init, B=B, S=S, online=True),
                      *attn_args)
        x1 = _merge(hm, P, ao, xf, w_m_out[l].astype(BF16), w_pool[l].astype(BF16),
                    pool_scale[l].reshape(1, D_MODEL), w_diff_out[l].astype(BF16),
                    w_out[l].astype(BF16), B, S)
        xf = _ffn(x1, g_ffn[l].reshape(1, D_MODEL), w_gate_up[l].astype(BF16),
                  w_down[l].astype(BF16))
    return xf.reshape(B, S, D_MODEL)
```

```python
import functools
import math

import jax
import jax.numpy as jnp
from jax import lax
from jax.experimental import pallas as pl
from jax.experimental.pallas import tpu as pltpu

D_MODEL = 1024
DEPTH = 2
M_HEADS = 4
M_DH = 256
M_CHUNK = 128
CONV_W = 4
POOL_WINDOWS = (2, 4, 8, 16)
POOL_G = 256
POOL_HALO = 16
A_HEADS = 8
A_DK = 64
A_DV = 128
FF = 2816
NORM_EPS = 1e-6
NEG_BIG = -1e30
ROPE_THETA = 10000.0

COL_MQ, COL_MK, COL_MV, COL_MO = 0, 1024, 2048, 3072
COL_PU, COL_AQ, COL_AK, COL_AV, COL_GATE = 4096, 5120, 6144, 7168, 8192
N_MAIN = 11264
GATE_ROWS = 16

BF16 = jnp.bfloat16
F32 = jnp.float32

VMEM_LIMIT = 56 * 1024 * 1024

PROJ_TM, PROJ_TN = 1024, 1024
PROJ_CHUNK = 256
MLSTM_CHUNKS_PER_STEP = 2
ATTN_T = 512
ATTN_MAX_STATIC_BOUND = 60.0
MERGE_TM = 512
POOL_BAND_ROWS = 256
FFN_TM = 256
FFN_CHUNKS = ((0, 512), (512, 1024), (1024, 1536), (1536, 2048), (2048, 2560), (2560, 2816))


def _params(n_axes):
    return pltpu.CompilerParams(dimension_semantics=("arbitrary",) * n_axes,
                                vmem_limit_bytes=VMEM_LIMIT)


def _sigmoid(x):
    return 1.0 / (1.0 + jnp.exp(-x))


def _proj_kernel(x_ref, g_ref, w_ref, wif_ref, bif_ref, cos_ref, sin_ref, gqk_ref, gm_ref,
                 out_ref, gates_ref, h_ref):
    j = pl.program_id(1)
    tm = x_ref.shape[0]
    C = PROJ_CHUNK
    L, H = M_CHUNK, M_HEADS

    @pl.when(j == 0)
    def _():
        x = x_ref[...]
        ms = jnp.mean(x * x, axis=-1, keepdims=True)
        hb = (x * lax.rsqrt(ms + NORM_EPS) * g_ref[...]).astype(BF16)
        h_ref[...] = hb
        gt = lax.dot_general(wif_ref[...], hb, (((1,), (1,)), ((), ())),
                             preferred_element_type=F32) + bif_ref[...]
        lf = jnp.minimum(gt, 0.0) - jnp.log(1.0 + jnp.exp(-jnp.abs(gt)))
        lane = lax.broadcasted_iota(jnp.int32, (GATE_ROWS, L), 1)
        is_input_gate = lax.broadcasted_iota(jnp.int32, (GATE_ROWS, L), 0) < H
        for c in range(tm // L):
            cols = slice(c * L, (c + 1) * L)
            b = lf[:, cols]
            for sh in (1, 2, 4, 8, 16, 32, 64):
                b = b + jnp.where(lane >= sh, pltpu.roll(b, sh, axis=1), 0.0)
            gates_ref[:, cols] = jnp.where(is_input_gate, gt[:, cols], b)

    def chunk_dot(r):
        return jnp.dot(h_ref[r * C:(r + 1) * C, :], w_ref[...], preferred_element_type=F32)

    is_sig = (j == 3) | (j >= 8)
    is_rope = (j == 5) | (j == 6)

    def chunked(epilogue):
        for r in range(tm // C):
            epilogue(r, chunk_dot(r))

    def store_plain(r, acc):
        out_ref[r * C:(r + 1) * C, :] = acc.astype(BF16)

    def store_sigmoid(r, acc):
        out_ref[r * C:(r + 1) * C, :] = _sigmoid(acc).astype(BF16)

    @pl.when(jnp.logical_not(is_sig | is_rope))
    def _():
        chunked(store_plain)

    @pl.when(is_sig)
    def _():
        chunked(store_sigmoid)

    @pl.when(is_rope)
    def _():
        g = jnp.where(j == 5, gqk_ref[0:1, :] * (A_DK ** -0.5 * math.log2(math.e)), gqk_ref[1:2, :])
        lane = lax.broadcasted_iota(jnp.int32, (C, 128), 1)
        first_half = (lane & (A_DK // 2)) == 0
        gm = gm_ref[...]

        def store_rope(r, acc):
            cos = cos_ref[r * C:(r + 1) * C, :]
            sin = sin_ref[r * C:(r + 1) * C, :]
            for c in range(A_HEADS):
                cols = slice(c * 128, (c + 1) * 128)
                x = acc[:, cols]
                ss = jnp.dot((x * x).astype(BF16), gm, preferred_element_type=F32)
                y = x * lax.rsqrt(ss * (1.0 / A_DK) + NORM_EPS) * g
                partner = jnp.where(first_half, pltpu.roll(y, 128 - A_DK // 2, axis=1),
                                    pltpu.roll(y, A_DK // 2, axis=1))
                out_ref[r * C:(r + 1) * C, cols] = (y * cos + partner * sin).astype(BF16)

        chunked(store_rope)


def _proj(xf, g, w_main, wif_t, bif, cos_t, sin_t, g2, gmat, S):
    T = xf.shape[0]
    tm, tn = PROJ_TM, PROJ_TN
    tps = S // tm
    const = lambda shape: pl.BlockSpec(shape, lambda i, j: (0, 0))
    return pl.pallas_call(
        _proj_kernel,
        grid=(T // tm, N_MAIN // tn),
        in_specs=[pl.BlockSpec((tm, D_MODEL), lambda i, j: (i, 0)),
                  const((1, D_MODEL)),
                  pl.BlockSpec((D_MODEL, tn), lambda i, j: (0, j)),
                  const((GATE_ROWS, D_MODEL)),
                  const((GATE_ROWS, 1)),
                  pl.BlockSpec((tm, 128), lambda i, j: (i % tps, 0)),
                  pl.BlockSpec((tm, 128), lambda i, j: (i % tps, 0)),
                  const((2, 128)), const((128, 128))],
        out_specs=[pl.BlockSpec((tm, tn), lambda i, j: (i, j)),
                   pl.BlockSpec((GATE_ROWS, tm), lambda i, j: (0, i))],
        out_shape=[jax.ShapeDtypeStruct((T, N_MAIN), BF16),
                   jax.ShapeDtypeStruct((GATE_ROWS, T), F32)],
        scratch_shapes=[pltpu.VMEM((tm, D_MODEL), BF16)],
        compiler_params=_params(2),
        name="proj",
    )(xf, g, w_main, wif_t, bif, cos_t, sin_t, g2, gmat)


def _mlstm_kernel(q_ref, k_ref, v_ref, o_ref, gt_ref, cw_ref, shift_ref, out_ref, ubuf, c_ref, n_ref,
                  m_ref):
    L, H, Dh = M_CHUNK, M_HEADS, M_DH
    W = H * Dh
    R = q_ref.shape[0]
    NC = R // L
    pairs = [(cc, h) for cc in range(NC) for h in range(H)]

    @pl.when(pl.program_id(1) == 0)
    def _():
        ubuf[0:L, :] = jnp.zeros((L, 2 * W), BF16)
        c_ref[...] = jnp.zeros(c_ref.shape, F32)
        n_ref[...] = jnp.zeros(n_ref.shape, F32)
        m_ref[...] = jnp.full(m_ref.shape, NEG_BIG, F32)

    ubuf[L:L + R, 0:W] = q_ref[...]
    ubuf[L:L + R, W:2 * W] = k_ref[...]

    row = lax.broadcasted_iota(jnp.int32, (L, L), 0)
    col = lax.broadcasted_iota(jnp.int32, (L, L), 1)
    eye = row == col
    tril = row >= col

    def to_col(r):
        return jnp.sum(jnp.where(eye, jnp.broadcast_to(r, (L, L)), 0.0), axis=1, keepdims=True)

    def conv_silu(r0, c0):
        cols = slice(c0, c0 + Dh)
        sh = jnp.dot(shift_ref[...], ubuf[r0:r0 + 2 * L, cols], preferred_element_type=F32)
        y = (cw_ref[3:4, cols] * sh[0:L] + cw_ref[2:3, cols] * sh[L:2 * L]
             + cw_ref[1:2, cols] * sh[2 * L:3 * L] + cw_ref[0:1, cols] * sh[3 * L:4 * L])
        return y / (1.0 + jnp.exp2(y * (-math.log2(math.e))))

    rows_of = lambda cc: slice(cc * L, (cc + 1) * L)
    cols_of = lambda h: slice(h * Dh, (h + 1) * Dh)

    g_tot, rowterm, m_loc, e_col, b_col = {}, {}, {}, {}, {}
    e_row, b_row = {}, {}
    for cc, h in pairs:
        li = gt_ref[h:h + 1, rows_of(cc)]
        b_row[cc, h] = gt_ref[H + h:H + h + 1, rows_of(cc)]
        g_tot[cc, h] = b_row[cc, h][:, L - 1:L]
        rowterm[cc, h] = li - b_row[cc, h]
        w_state = g_tot[cc, h] + rowterm[cc, h]
        m_loc[cc, h] = jnp.max(w_state, axis=1, keepdims=True)
        e_row[cc, h] = jnp.exp(w_state - m_loc[cc, h])
    for cc, h in pairs:
        e_col[cc, h] = to_col(e_row[cc, h])
        b_col[cc, h] = to_col(b_row[cc, h])

    m_prev, a_dec, b_in = {}, {}, {}
    for h in range(H):
        m = m_ref[h:h + 1, 0:1]
        for cc in range(NC):
            m_prev[cc, h] = m
            m_new = jnp.maximum(g_tot[cc, h] + m, m_loc[cc, h])
            a_dec[cc, h] = jnp.exp(g_tot[cc, h] + m - m_new)
            b_in[cc, h] = jnp.exp(m_loc[cc, h] - m_new)
            m = m_new
        m_ref[h:h + 1, :] = jnp.broadcast_to(m, (1, 128))

    qh, kh, qb, kb = {}, {}, {}, {}
    for cc, h in pairs:
        qh[cc, h] = conv_silu(cc * L, h * Dh) * (Dh ** -0.5)
        kh[cc, h] = conv_silu(cc * L, W + h * Dh)
        qb[cc, h] = qh[cc, h].astype(BF16)
        kb[cc, h] = kh[cc, h].astype(BF16)

    raw = {}
    for cc, h in pairs:
        raw[cc, h] = lax.dot_general(qb[cc, h], kb[cc, h], (((1,), (1,)), ((), ())),
                                     preferred_element_type=F32)
    inter, sv, den0, floor = {}, {}, {}, {}
    for cc, h in pairs:
        dm = jnp.where(tril, b_col[cc, h] + rowterm[cc, h], NEG_BIG)
        m_intra = jnp.max(dm, axis=1, keepdims=True)
        m_inter = b_col[cc, h] + m_prev[cc, h]
        m_t = jnp.maximum(m_inter, m_intra)
        sqk = raw[cc, h] * jnp.exp(dm - m_t)
        inter[cc, h] = jnp.exp(m_inter - m_t)
        floor[cc, h] = jnp.exp(-m_t)
        den0[cc, h] = jnp.sum(sqk, axis=1, keepdims=True)
        sv[cc, h] = jnp.dot(sqk.astype(BF16), v_ref[rows_of(cc), cols_of(h)],
                            preferred_element_type=F32)

    for cc in range(NC):
        c_loc, n_loc = {}, {}
        for h in range(H):
            ke = kh[cc, h] * e_col[cc, h]
            c_loc[h] = lax.dot_general(ke.astype(BF16), v_ref[rows_of(cc), cols_of(h)],
                                       (((0,), (0,)), ((), ())), preferred_element_type=F32)
            n_loc[h] = jnp.sum(ke, axis=0, keepdims=True)
        for h in range(H):
            c_prev = c_ref[h]
            n_prev = n_ref[h:h + 1, :]
            q_c = jnp.dot(qb[cc, h], c_prev.astype(BF16), preferred_element_type=F32)
            q_n = jnp.sum(qh[cc, h] * n_prev, axis=1, keepdims=True)
            num = sv[cc, h] + inter[cc, h] * q_c
            den = den0[cc, h] + inter[cc, h] * q_n
            hh = num / jnp.maximum(jnp.abs(den), floor[cc, h])
            out_ref[rows_of(cc), cols_of(h)] = (
                o_ref[rows_of(cc), cols_of(h)].astype(F32) * hh).astype(BF16)
            c_ref[h] = a_dec[cc, h] * c_prev + b_in[cc, h] * c_loc[h]
            n_ref[h:h + 1, :] = a_dec[cc, h] * n_prev + b_in[cc, h] * n_loc[h]

    ubuf[0:L, :] = ubuf[R:R + L, :]


def _mlstm(P, gates, conv_w, B, S):
    T = B * S
    L, H, Dh = M_CHUNK, M_HEADS, M_DH
    R = MLSTM_CHUNKS_PER_STEP * L
    ns = S // R
    W = H * Dh
    blk = lambda cb: pl.BlockSpec((R, W), lambda b, c: (b * ns + c, cb))
    t_idx = jnp.arange(CONV_W * L) % L
    s_idx = jnp.arange(CONV_W * L) // L
    shift = (jnp.arange(2 * L)[None, :] == (L + t_idx - s_idx)[:, None]).astype(BF16)
    return pl.pallas_call(
        _mlstm_kernel,
        grid=(B, ns),
        in_specs=[blk(COL_MQ // W), blk(COL_MK // W), blk(COL_MV // W), blk(COL_MO // W),
                  pl.BlockSpec((GATE_ROWS, R), lambda b, c: (0, b * ns + c)),
                  pl.BlockSpec((CONV_W, 2 * W), lambda b, c: (0, 0)),
                  pl.BlockSpec((CONV_W * L, 2 * L), lambda b, c: (0, 0))],
        out_specs=pl.BlockSpec((R, W), lambda b, c: (b * ns + c, 0)),
        out_shape=jax.ShapeDtypeStruct((T, W), BF16),
        scratch_shapes=[pltpu.VMEM((L + R, 2 * W), BF16),
                        pltpu.VMEM((H, Dh, Dh), F32),
                        pltpu.VMEM((8, Dh), F32),
                        pltpu.VMEM((8, 128), F32)],
        compiler_params=_params(2),
        name="mlstm",
    )(P, P, P, P, gates, conv_w, shift)


def _attn_kernel(*refs, lam_init, online):
    if online:
        q_ref, k_ref, v_ref, lam_ref, gd_ref, out_ref, qs_ref, vt_ref, l_ref, acc_ref, m_ref = refs
    else:
        c_ref, q_ref, k_ref, v_ref, lam_ref, gd_ref, out_ref, qs_ref, vt_ref, l_ref, acc_ref = refs
    t = q_ref.shape[0]
    nkv = k_ref.shape[0] // t
    qi = pl.program_id(2)

    @pl.when(qi == 0)
    def _():
        for c in range(nkv):
            vt_ref[:, c * t:(c + 1) * t] = v_ref[c * t:(c + 1) * t, :].astype(F32).T.astype(BF16)

    q_t = q_ref[...].astype(F32).T
    sub = lax.broadcasted_iota(jnp.int32, (128, t), 0)
    qs_ref[:, 0:t] = jnp.where(sub < A_DK, q_t, 0.0).astype(BF16)
    qs_ref[:, t:2 * t] = jnp.where(sub >= A_DK, q_t, 0.0).astype(BF16)
    l_ref[...] = jnp.zeros(l_ref.shape, F32)
    acc_ref[...] = jnp.zeros(acc_ref.shape, F32)
    if online:
        m_ref[...] = jnp.full(m_ref.shape, NEG_BIG, F32)

    def step(j, nb, diag_last):
        start = pl.multiple_of(j * t, t)
        k = k_ref[pl.ds(start, nb * t), :]
        vt = vt_ref[:, pl.ds(start, nb * t)]
        s = jnp.dot(k, qs_ref[...], preferred_element_type=F32)
        if diag_last:
            row = lax.broadcasted_iota(jnp.int32, (nb * t, 2 * t), 0) - (nb - 1) * t
            col = lax.broadcasted_iota(jnp.int32, (nb * t, 2 * t), 1)
            qcol = jnp.where(col >= t, col - t, col)
            s = jnp.where(row <= qcol, s, NEG_BIG)
        if online:
            m_prev = m_ref[...]
            m_new = jnp.maximum(m_prev, jnp.max(s, axis=0, keepdims=True))
            alpha = jnp.exp2(m_prev - m_new)
            p = jnp.exp2(s - m_new)
            l_ref[...] = alpha * l_ref[...] + jnp.sum(p, axis=0, keepdims=True)
            acc_ref[...] = alpha * acc_ref[...] + jnp.dot(vt, p.astype(BF16),
                                                          preferred_element_type=F32)
            m_ref[...] = m_new
        else:
            p = jnp.exp2(s - c_ref[0])
            l_ref[...] += jnp.sum(p, axis=0, keepdims=True)
            acc_ref[...] += jnp.dot(vt, p.astype(BF16), preferred_element_type=F32)

    def body(jj, carry):
        step(2 * jj, 2, False)
        return carry

    lax.fori_loop(0, qi // 2, body, 0)

    @pl.when(qi % 2 == 1)
    def _():
        step(qi - 1, 2, True)

    @pl.when(qi % 2 == 0)
    def _():
        step(qi, 1, True)

    accn = acc_ref[...] / l_ref[...]
    lp = lam_ref[...]
    lam = (jnp.exp(jnp.sum(lp[0:1, :] * lp[1:2, :], axis=1, keepdims=True))
           - jnp.exp(jnp.sum(lp[2:3, :] * lp[3:4, :], axis=1, keepdims=True)) + lam_init)
    o_t = accn[:, 0:t] - lam * accn[:, t:2 * t]
    ms = jnp.mean(o_t * o_t, axis=0, keepdims=True)
    o = (o_t * lax.rsqrt(ms + NORM_EPS)).T
    out_ref[...] = (o * (gd_ref[...] * (1.0 - lam_init))).astype(BF16)


def _attn(c, P, lam_p, gd, *, lam_init, B, S, online):
    T = B * S
    t = ATTN_T
    nq = S // t
    in_specs = [pl.BlockSpec((t, 128), lambda b, h, i: (b * nq + i, COL_AQ // 128 + h)),
                pl.BlockSpec((S, 128), lambda b, h, i: (b, COL_AK // 128 + h)),
                pl.BlockSpec((S, 128), lambda b, h, i: (b, COL_AV // 128 + h)),
                pl.BlockSpec((4, A_DK), lambda b, h, i: (0, 0)),
                pl.BlockSpec((1, A_DV), lambda b, h, i: (0, 0))]
    scratch = [pltpu.VMEM((128, 2 * t), BF16),
               pltpu.VMEM((128, S), BF16),
               pltpu.VMEM((1, 2 * t), F32),
               pltpu.VMEM((128, 2 * t), F32)]
    args = (P, P, P, lam_p, gd)
    if online:
        scratch = scratch + [pltpu.VMEM((1, 2 * t), F32)]
    else:
        in_specs = [pl.BlockSpec(memory_space=pltpu.SMEM)] + in_specs
        args = (c,) + args
    return pl.pallas_call(
        functools.partial(_attn_kernel, lam_init=lam_init, online=online),
        grid=(B, A_HEADS, nq),
        in_specs=in_specs,
        out_specs=pl.BlockSpec((t, 128), lambda b, h, i: (b * nq + i, h)),
        out_shape=jax.ShapeDtypeStruct((T, A_HEADS * A_DV), BF16),
        scratch_shapes=scratch,
        compiler_params=_params(3),
        name="attn_online" if online else "attn",
    )(*args)


def _merge_kernel(hm_ref, pu_ref, ao_ref, gm_ref, gp_ref, ga_ref, x_ref, wm_ref, wp_ref, ps_ref,
                  wd_ref, wo_ref, bm_ref, bh_ref, out_ref, halo_ref, mrg_ref, win_ref,
                  *, tiles_per_seq):
    tm = x_ref.shape[0]
    G = POOL_G
    i = pl.program_id(0)
    tile_in_seq = i % tiles_per_seq

    @pl.when(tile_in_seq == 0)
    def _():
        halo_ref[...] = jnp.zeros(halo_ref.shape, BF16)

    pos1 = tile_in_seq * tm + lax.broadcasted_iota(jnp.int32, (tm, 1), 0) + 1

    tb = bm_ref.shape[1]
    n_win = len(POOL_WINDOWS)
    for r in range(tm // tb):
        for gi in range(n_win):
            cols = slice(gi * G, (gi + 1) * G)
            prev = halo_ref[:, cols] if r == 0 else pu_ref[r * tb - POOL_HALO:r * tb, cols]
            win = win_ref.at[r * n_win + gi]
            win[...] = jnp.dot(bm_ref[gi], pu_ref[r * tb:(r + 1) * tb, cols],
                               preferred_element_type=F32)
            win[0:POOL_HALO, :] += jnp.dot(bh_ref[gi], prev, preferred_element_type=F32)

    y_m = jnp.dot(hm_ref[...], wm_ref[...], preferred_element_type=F32)
    y_a = jnp.dot(ao_ref[...], wd_ref[...], preferred_element_type=F32)
    mrg_ref[...] = gm_ref[...].astype(F32) * y_m + ga_ref[...].astype(F32) * y_a

    for r in range(tm // tb):
        rows = slice(r * tb, (r + 1) * tb)
        for gi, w in enumerate(POOL_WINDOWS):
            cols = slice(gi * G, (gi + 1) * G)
            cnt = jnp.minimum(pos1[rows], w).astype(F32)
            pooled = win_ref[r * n_win + gi] / cnt - pu_ref[rows, cols].astype(F32)
            y_p = (jnp.dot(pooled.astype(BF16), wp_ref[gi], preferred_element_type=F32)
                   * ps_ref[:, cols])
            mrg_ref[rows, cols] += gp_ref[rows, cols].astype(F32) * y_p
    halo_ref[...] = pu_ref[tm - POOL_HALO:tm, :]

    out_ref[...] = x_ref[...] + jnp.dot(mrg_ref[...].astype(BF16), wo_ref[...],
                                        preferred_element_type=F32)


def _pool_bands(tm):
    t_idx = jnp.arange(tm)[:, None]
    main, halo = [], []
    for w in POOL_WINDOWS:
        d_main = t_idx - jnp.arange(tm)[None, :]
        d_halo = t_idx[:POOL_HALO] - (jnp.arange(POOL_HALO)[None, :] - POOL_HALO)
        main.append((d_main >= 0) & (d_main < w))
        halo.append((d_halo >= 0) & (d_halo < w))
    return jnp.stack(main).astype(BF16), jnp.stack(halo).astype(BF16)


def _merge(hm, P, ao, xf, wm, wp, ps, wd, wo, B, S):
    T = B * S
    tm = MERGE_TM
    tb = POOL_BAND_ROWS
    band_main, band_halo = _pool_bands(tb)
    row = lambda w, cb: pl.BlockSpec((tm, w), lambda i: (i, cb))
    full = lambda shape: pl.BlockSpec(shape, lambda i: (0,) * len(shape),
                                      pipeline_mode=pl.Buffered(1))
    return pl.pallas_call(
        functools.partial(_merge_kernel, tiles_per_seq=S // tm),
        grid=(T // tm,),
        in_specs=[row(D_MODEL, 0), row(D_MODEL, COL_PU // D_MODEL), row(D_MODEL, 0),
                  row(D_MODEL, COL_GATE // D_MODEL), row(D_MODEL, COL_GATE // D_MODEL + 1),
                  row(D_MODEL, COL_GATE // D_MODEL + 2), row(D_MODEL, 0),
                  full((D_MODEL, D_MODEL)), full((4, POOL_G, POOL_G)), full((1, D_MODEL)),
                  full((D_MODEL, D_MODEL)), full((D_MODEL, D_MODEL)),
                  full((4, tb, tb)), full((4, POOL_HALO, POOL_HALO))],
        out_specs=row(D_MODEL, 0),
        out_shape=jax.ShapeDtypeStruct((T, D_MODEL), F32),
        scratch_shapes=[pltpu.VMEM((POOL_HALO, D_MODEL), BF16),
                        pltpu.VMEM((tm, D_MODEL), F32),
                        pltpu.VMEM((tm // tb * len(POOL_WINDOWS), tb, POOL_G), F32)],
        compiler_params=_params(1),
        name="merge",
    )(hm, P, ao, P, P, P, xf, wm, wp, ps, wd, wo, band_main, band_halo)


def _ffn_kernel(x_ref, g_ref, wgu_ref, wdn_ref, out_ref, act_ref):
    x = x_ref[...]
    ms = jnp.mean(x * x, axis=-1, keepdims=True)
    hb = (x * lax.rsqrt(ms + NORM_EPS) * g_ref[...]).astype(BF16)
    for lo, hi in FFN_CHUNKS:
        gate = jnp.dot(hb, wgu_ref[:, lo:hi], preferred_element_type=F32)
        up = jnp.dot(hb, wgu_ref[:, FF + lo:FF + hi], preferred_element_type=F32)
        act_ref[:, lo:hi] = (gate * _sigmoid(gate) * up).astype(BF16)
    out_ref[...] = x + jnp.dot(act_ref[...], wdn_ref[...], preferred_element_type=F32)


def _ffn(xf, g, wgu, wdn):
    T = xf.shape[0]
    tm = FFN_TM
    const = lambda shape: pl.BlockSpec(shape, lambda i: (0, 0), pipeline_mode=pl.Buffered(1))
    return pl.pallas_call(
        _ffn_kernel,
        grid=(T // tm,),
        in_specs=[pl.BlockSpec((tm, D_MODEL), lambda i: (i, 0)),
                  pl.BlockSpec((1, D_MODEL), lambda i: (0, 0)),
                  const((D_MODEL, 2 * FF)), const((FF, D_MODEL))],
        out_specs=pl.BlockSpec((tm, D_MODEL), lambda i: (i, 0)),
        out_shape=jax.ShapeDtypeStruct((T, D_MODEL), F32),
        scratch_shapes=[pltpu.VMEM((tm, FF), BF16)],
        compiler_params=_params(1),
        name="ffn",
    )(xf, g, wgu, wdn)


def _rope_tables(S):
    half = A_DK // 2
    inv = ROPE_THETA ** (-jnp.arange(half, dtype=F32) / half)
    ang = jnp.arange(S, dtype=F32)[:, None] * inv[None, :]
    cos, sin = jnp.cos(ang), jnp.sin(ang)
    cos_t = jnp.concatenate([cos, cos, cos, cos], axis=1)
    sin_t = jnp.concatenate([-sin, sin, -sin, sin], axis=1)
    return cos_t, sin_t


def kernel(x, g_mix, w_in, b_if, conv_qk, w_m_out, w_pool, pool_scale, g_qk, lam_p, g_diff_head,
           w_diff_out, w_out, g_ffn, w_gate_up, w_down):
    B, S, _ = x.shape
    T = B * S
    xf = x.reshape(T, D_MODEL)
    cos_t, sin_t = _rope_tables(S)
    lane = jnp.arange(128)
    gmat = (lane[:, None] // A_DK == lane[None, :] // A_DK).astype(BF16)
    n_gate = 2 * M_HEADS
    gate_lo = 4 * M_HEADS * M_DH

    for l in range(DEPTH):
        wl = w_in[l]
        w_main = jnp.concatenate([wl[:, :gate_lo], wl[:, gate_lo + n_gate:]], axis=1).astype(BF16)
        wif_t = jnp.pad(wl[:, gate_lo:gate_lo + n_gate].T, ((0, GATE_ROWS - n_gate), (0, 0))).astype(BF16)
        bif = jnp.pad(b_if[l], (0, GATE_ROWS - n_gate)).reshape(GATE_ROWS, 1)
        g2 = jnp.concatenate([g_qk[l], g_qk[l]], axis=1)
        lam_init = 0.8 - 0.6 * math.exp(-0.3 * l)

        P, gates = _proj(xf, g_mix[l].reshape(1, D_MODEL), w_main, wif_t, bif, cos_t, sin_t,
                         g2, gmat, S)
        hm = _mlstm(P, gates, conv_qk[l], B, S)
        c = (A_DK ** 0.5 * math.log2(math.e)) * jnp.max(jnp.abs(g_qk[l][0])) * jnp.max(jnp.abs(g_qk[l][1]))
        attn_args = (c.reshape(1), P, lam_p[l], g_diff_head[l].reshape(1, A_DV))
        ao = lax.cond(c < ATTN_MAX_STATIC_BOUND,
                      functools.partial(_attn, lam_init=lam_init, B=B, S=S, online=False),
                      functools.partial(_attn, lam_init=lam_init, B=B, S=S, online=True),
                      *attn_args)
        x1 = _merge(hm, P, ao, xf, w_m_out[l].astype(BF16), w_pool[l].astype(BF16),
                    pool_scale[l].reshape(1, D_MODEL), w_diff_out[l].astype(BF16),
                    w_out[l].astype(BF16), B, S)
        xf = _ffn(x1, g_ffn[l].reshape(1, D_MODEL), w_gate_up[l].astype(BF16),
                  w_down[l].astype(BF16))
    return xf.reshape(B, S, D_MODEL)
```

```python
import functools
import math

import jax
import jax.numpy as jnp
from jax import lax
from jax.experimental import pallas as pl
from jax.experimental.pallas import tpu as pltpu

D_MODEL = 1024
DEPTH = 2
M_HEADS = 4
M_DH = 256
M_CHUNK = 128
CONV_W = 4
POOL_WINDOWS = (2, 4, 8, 16)
POOL_G = 256
POOL_HALO = 16
A_HEADS = 8
A_DK = 64
A_DV = 128
FF = 2816
NORM_EPS = 1e-6
NEG_BIG = -1e30
ROPE_THETA = 10000.0

COL_MQ, COL_MK, COL_MV, COL_MO = 0, 1024, 2048, 3072
COL_PU, COL_AQ, COL_AK, COL_AV, COL_GATE = 4096, 5120, 6144, 7168, 8192
N_MAIN = 11264
GATE_ROWS = 16

BF16 = jnp.bfloat16
F32 = jnp.float32

VMEM_LIMIT = 56 * 1024 * 1024

PROJ_TM, PROJ_TN = 1024, 1024
PROJ_CHUNK = 256
MLSTM_CHUNKS_PER_STEP = 4
ATTN_T = 512
ATTN_BLOCKS_PER_TRIP = 4
ATTN_MAX_STATIC_BOUND = 60.0
MERGE_TM = 512
POOL_BAND_ROWS = 256
FFN_TM = 256
FFN_CHUNKS = ((0, 512), (512, 1024), (1024, 1536), (1536, 2048), (2048, 2560), (2560, 2816))


def _params(n_axes):
    return pltpu.CompilerParams(dimension_semantics=("arbitrary",) * n_axes,
                                vmem_limit_bytes=VMEM_LIMIT)


def _sigmoid(x):
    return 1.0 / (1.0 + jnp.exp(-x))


def _proj_kernel(x_ref, g_ref, w_ref, wif_ref, bif_ref, cos_ref, sin_ref, gqk_ref, gm_ref,
                 out_ref, gates_ref, h_ref):
    j = pl.program_id(1)
    tm = x_ref.shape[0]
    C = PROJ_CHUNK
    L, H = M_CHUNK, M_HEADS

    @pl.when(j == 0)
    def _():
        lane = lax.broadcasted_iota(jnp.int32, (GATE_ROWS, L), 1)
        is_input_gate = lax.broadcasted_iota(jnp.int32, (GATE_ROWS, L), 0) < H
        for r in range(tm // C):
            rows = slice(r * C, (r + 1) * C)
            x = x_ref[rows, :]
            ms = jnp.mean(x * x, axis=-1, keepdims=True)
            hb = (x * lax.rsqrt(ms + NORM_EPS) * g_ref[...]).astype(BF16)
            h_ref[rows, :] = hb
            out_ref[rows, :] = jnp.dot(hb, w_ref[...], preferred_element_type=F32).astype(BF16)
            gt = lax.dot_general(wif_ref[...], hb, (((1,), (1,)), ((), ())),
                                 preferred_element_type=F32) + bif_ref[...]
            lf = jnp.minimum(gt, 0.0) - jnp.log(1.0 + jnp.exp(-jnp.abs(gt)))
            for c in range(C // L):
                b = lf[:, c * L:(c + 1) * L]
                for sh in (1, 2, 4, 8, 16, 32, 64):
                    b = b + jnp.where(lane >= sh, pltpu.roll(b, sh, axis=1), 0.0)
                gates_ref[:, r * C + c * L:r * C + (c + 1) * L] = jnp.where(
                    is_input_gate, gt[:, c * L:(c + 1) * L], b)

    def chunk_dot(r):
        return jnp.dot(h_ref[r * C:(r + 1) * C, :], w_ref[...], preferred_element_type=F32)

    is_sig = (j == 3) | (j >= 8)
    is_rope = (j == 5) | (j == 6)

    def chunked(epilogue):
        for r in range(tm // C):
            epilogue(r, chunk_dot(r))

    def store_plain(r, acc):
        out_ref[r * C:(r + 1) * C, :] = acc.astype(BF16)

    def store_sigmoid(r, acc):
        out_ref[r * C:(r + 1) * C, :] = _sigmoid(acc).astype(BF16)

    @pl.when(jnp.logical_not(is_sig | is_rope | (j == 0)))
    def _():
        chunked(store_plain)

    @pl.when(is_sig)
    def _():
        chunked(store_sigmoid)

    @pl.when(is_rope)
    def _():
        g = jnp.where(j == 5, gqk_ref[0:1, :] * (A_DK ** -0.5 * math.log2(math.e)), gqk_ref[1:2, :])
        lane = lax.broadcasted_iota(jnp.int32, (C, 128), 1)
        first_half = (lane & (A_DK // 2)) == 0
        gm = gm_ref[...]

        def store_rope(r, acc):
            cos = cos_ref[r * C:(r + 1) * C, :]
            sin = sin_ref[r * C:(r + 1) * C, :]
            for c2 in range(A_HEADS // 2):
                x2 = acc[:, c2 * 256:(c2 + 1) * 256]
                ss2 = jnp.dot((x2 * x2).astype(BF16), gm, preferred_element_type=F32)
                y2 = x2 * lax.rsqrt(ss2 * (1.0 / A_DK) + NORM_EPS)
                for c in (2 * c2, 2 * c2 + 1):
                    cols = slice(c * 128, (c + 1) * 128)
                    y = y2[:, (c % 2) * 128:(c % 2 + 1) * 128] * g
                    partner = jnp.where(first_half, pltpu.roll(y, 128 - A_DK // 2, axis=1),
                                        pltpu.roll(y, A_DK // 2, axis=1))
                    out_ref[r * C:(r + 1) * C, cols] = (y * cos + partner * sin).astype(BF16)

        chunked(store_rope)


def _proj(xf, g, w_main, wif_t, bif, cos_t, sin_t, g2, gmat, S, l):
    T = xf.shape[0]
    tm, tn = PROJ_TM, PROJ_TN
    tps = S // tm
    const = lambda shape: pl.BlockSpec(shape, lambda i, j: (0, 0))
    return pl.pallas_call(
        _proj_kernel,
        grid=(T // tm, N_MAIN // tn),
        in_specs=[pl.BlockSpec((tm, D_MODEL), lambda i, j: (i, 0)),
                  const((1, D_MODEL)),
                  pl.BlockSpec((None, D_MODEL, tn), lambda i, j: (l, 0, j)),
                  pl.BlockSpec((None, GATE_ROWS, D_MODEL), lambda i, j: (l, 0, 0)),
                  const((GATE_ROWS, 1)),
                  pl.BlockSpec((tm, 128), lambda i, j: (i % tps, 0)),
                  pl.BlockSpec((tm, 128), lambda i, j: (i % tps, 0)),
                  const((2, 128)), const((256, 256))],
        out_specs=[pl.BlockSpec((tm, tn), lambda i, j: (i, j)),
                   pl.BlockSpec((GATE_ROWS, tm), lambda i, j: (0, i))],
        out_shape=[jax.ShapeDtypeStruct((T, N_MAIN), BF16),
                   jax.ShapeDtypeStruct((GATE_ROWS, T), F32)],
        scratch_shapes=[pltpu.VMEM((tm, D_MODEL), BF16)],
        compiler_params=_params(2),
        name="proj",
    )(xf, g, w_main, wif_t, bif, cos_t, sin_t, g2, gmat)


def _mlstm_kernel(q_ref, k_ref, v_ref, o_ref, gt_ref, cw_ref, shift_ref, out_ref, ubuf, c_ref, n_ref,
                  m_ref):
    L, H, Dh = M_CHUNK, M_HEADS, M_DH
    W = H * Dh
    R = q_ref.shape[0]
    NC = R // L
    pairs = [(cc, h) for cc in range(NC) for h in range(H)]

    @pl.when(pl.program_id(1) == 0)
    def _():
        ubuf[0:L, :] = jnp.zeros((L, 2 * W), BF16)
        c_ref[...] = jnp.zeros(c_ref.shape, F32)
        n_ref[...] = jnp.zeros(n_ref.shape, F32)
        m_ref[...] = jnp.full(m_ref.shape, NEG_BIG, F32)

    ubuf[L:L + R, 0:W] = q_ref[...]
    ubuf[L:L + R, W:2 * W] = k_ref[...]

    row = lax.broadcasted_iota(jnp.int32, (L, L), 0)
    col = lax.broadcasted_iota(jnp.int32, (L, L), 1)
    eye = row == col
    tril = row >= col

    def to_col(r):
        return jnp.sum(jnp.where(eye, jnp.broadcast_to(r, (L, L)), 0.0), axis=1, keepdims=True)

    def conv_silu(r0, c0):
        cols = slice(c0, c0 + Dh)
        sh = jnp.dot(shift_ref[...], ubuf[r0:r0 + 2 * L, cols], preferred_element_type=F32)
        y = (cw_ref[3:4, cols] * sh[0:L] + cw_ref[2:3, cols] * sh[L:2 * L]
             + cw_ref[1:2, cols] * sh[2 * L:3 * L] + cw_ref[0:1, cols] * sh[3 * L:4 * L])
        return y / (1.0 + jnp.exp2(y * (-math.log2(math.e))))

    rows_of = lambda cc: slice(cc * L, (cc + 1) * L)
    cols_of = lambda h: slice(h * Dh, (h + 1) * Dh)

    g_tot, rowterm, m_loc, e_col, b_col = {}, {}, {}, {}, {}
    e_row, b_row = {}, {}
    for cc, h in pairs:
        li = gt_ref[h:h + 1, rows_of(cc)]
        b_row[cc, h] = gt_ref[H + h:H + h + 1, rows_of(cc)]
        g_tot[cc, h] = b_row[cc, h][:, L - 1:L]
        rowterm[cc, h] = li - b_row[cc, h]
        w_state = g_tot[cc, h] + rowterm[cc, h]
        m_loc[cc, h] = jnp.max(w_state, axis=1, keepdims=True)
        e_row[cc, h] = jnp.exp(w_state - m_loc[cc, h])
    for cc, h in pairs:
        e_col[cc, h] = to_col(e_row[cc, h])
        b_col[cc, h] = to_col(b_row[cc, h])

    m_prev, a_dec, b_in = {}, {}, {}
    for h in range(H):
        m = m_ref[h:h + 1, 0:1]
        for cc in range(NC):
            m_prev[cc, h] = m
            m_new = jnp.maximum(g_tot[cc, h] + m, m_loc[cc, h])
            a_dec[cc, h] = jnp.exp(g_tot[cc, h] + m - m_new)
            b_in[cc, h] = jnp.exp(m_loc[cc, h] - m_new)
            m = m_new
        m_ref[h:h + 1, :] = jnp.broadcast_to(m, (1, 128))

    qh, kh, qb, kb = {}, {}, {}, {}
    for cc, h in pairs:
        qh[cc, h] = conv_silu(cc * L, h * Dh) * (Dh ** -0.5)
        kh[cc, h] = conv_silu(cc * L, W + h * Dh)
        qb[cc, h] = qh[cc, h].astype(BF16)
        kb[cc, h] = kh[cc, h].astype(BF16)

    raw = {}
    for cc, h in pairs:
        raw[cc, h] = lax.dot_general(qb[cc, h], kb[cc, h], (((1,), (1,)), ((), ())),
                                     preferred_element_type=F32)
    inter, sv, den0, floor = {}, {}, {}, {}
    for cc, h in pairs:
        dm = jnp.where(tril, b_col[cc, h] + rowterm[cc, h], NEG_BIG)
        m_intra = jnp.max(dm, axis=1, keepdims=True)
        m_inter = b_col[cc, h] + m_prev[cc, h]
        m_t = jnp.maximum(m_inter, m_intra)
        sqk = raw[cc, h] * jnp.exp(dm - m_t)
        inter[cc, h] = jnp.exp(m_inter - m_t)
        floor[cc, h] = jnp.exp(-m_t)
        den0[cc, h] = jnp.sum(sqk, axis=1, keepdims=True)
        sv[cc, h] = jnp.dot(sqk.astype(BF16), v_ref[rows_of(cc), cols_of(h)],
                            preferred_element_type=F32)

    for cc in range(NC):
        c_loc, n_loc = {}, {}
        for h in range(H):
            ke = kh[cc, h] * e_col[cc, h]
            c_loc[h] = lax.dot_general(ke.astype(BF16), v_ref[rows_of(cc), cols_of(h)],
                                       (((0,), (0,)), ((), ())), preferred_element_type=F32)
            n_loc[h] = jnp.sum(ke, axis=0, keepdims=True)
        for h in range(H):
            c_prev = c_ref[h]
            n_prev = n_ref[h:h + 1, :]
            q_c = jnp.dot(qb[cc, h], c_prev.astype(BF16), preferred_element_type=F32)
            q_n = jnp.sum(qh[cc, h] * n_prev, axis=1, keepdims=True)
            num = sv[cc, h] + inter[cc, h] * q_c
            den = den0[cc, h] + inter[cc, h] * q_n
            hh = num / jnp.maximum(jnp.abs(den), floor[cc, h])
            out_ref[rows_of(cc), cols_of(h)] = (
                o_ref[rows_of(cc), cols_of(h)].astype(F32) * hh).astype(BF16)
            c_ref[h] = a_dec[cc, h] * c_prev + b_in[cc, h] * c_loc[h]
            n_ref[h:h + 1, :] = a_dec[cc, h] * n_prev + b_in[cc, h] * n_loc[h]

    ubuf[0:L, :] = ubuf[R:R + L, :]


def _mlstm(P, gates, conv_w, B, S):
    T = B * S
    L, H, Dh = M_CHUNK, M_HEADS, M_DH
    R = MLSTM_CHUNKS_PER_STEP * L
    ns = S // R
    W = H * Dh
    blk = lambda cb: pl.BlockSpec((R, W), lambda b, c: (b * ns + c, cb))
    t_idx = jnp.arange(CONV_W * L) % L
    s_idx = jnp.arange(CONV_W * L) // L
    shift = (jnp.arange(2 * L)[None, :] == (L + t_idx - s_idx)[:, None]).astype(BF16)
    return pl.pallas_call(
        _mlstm_kernel,
        grid=(B, ns),
        in_specs=[blk(COL_MQ // W), blk(COL_MK // W), blk(COL_MV // W), blk(COL_MO // W),
                  pl.BlockSpec((GATE_ROWS, R), lambda b, c: (0, b * ns + c)),
                  pl.BlockSpec((CONV_W, 2 * W), lambda b, c: (0, 0)),
                  pl.BlockSpec((CONV_W * L, 2 * L), lambda b, c: (0, 0))],
        out_specs=pl.BlockSpec((R, W), lambda b, c: (b * ns + c, 0)),
        out_shape=jax.ShapeDtypeStruct((T, W), BF16),
        scratch_shapes=[pltpu.VMEM((L + R, 2 * W), BF16),
                        pltpu.VMEM((H, Dh, Dh), F32),
                        pltpu.VMEM((8, Dh), F32),
                        pltpu.VMEM((8, 128), F32)],
        compiler_params=_params(2),
        name="mlstm",
    )(P, P, P, P, gates, conv_w, shift)


def _attn_kernel(*refs, lam_init, online):
    if online:
        q_ref, k_ref, v_ref, lam_ref, gd_ref, out_ref, qs_ref, vt_ref, l_ref, acc_ref, m_ref = refs
    else:
        c_ref, q_ref, k_ref, v_ref, lam_ref, gd_ref, out_ref, qs_ref, vt_ref, l_ref, acc_ref = refs
    t = q_ref.shape[0]
    nkv = k_ref.shape[0] // t
    qi = pl.program_id(2)

    @pl.when(qi == 0)
    def _():
        for c in range(nkv):
            vt_ref[:, c * t:(c + 1) * t] = v_ref[c * t:(c + 1) * t, :].astype(F32).T.astype(BF16)

    q_t = q_ref[...].astype(F32).T
    sub = lax.broadcasted_iota(jnp.int32, (128, t), 0)
    qs_ref[:, 0:t] = jnp.where(sub < A_DK, q_t, 0.0).astype(BF16)
    qs_ref[:, t:2 * t] = jnp.where(sub >= A_DK, q_t, 0.0).astype(BF16)
    l_ref[...] = jnp.zeros(l_ref.shape, F32)
    acc_ref[...] = jnp.zeros(acc_ref.shape, F32)
    if online:
        m_ref[...] = jnp.full(m_ref.shape, NEG_BIG, F32)

    def step(j, nb, diag_last):
        start = pl.multiple_of(j * t, t)
        k = k_ref[pl.ds(start, nb * t), :]
        vt = vt_ref[:, pl.ds(start, nb * t)]
        s = jnp.dot(k, qs_ref[...], preferred_element_type=F32)
        if diag_last:
            row = lax.broadcasted_iota(jnp.int32, (nb * t, 2 * t), 0) - (nb - 1) * t
            col = lax.broadcasted_iota(jnp.int32, (nb * t, 2 * t), 1)
            qcol = jnp.where(col >= t, col - t, col)
            s = jnp.where(row <= qcol, s, NEG_BIG)
        if online:
            m_prev = m_ref[...]
            m_new = jnp.maximum(m_prev, jnp.max(s, axis=0, keepdims=True))
            alpha = jnp.exp2(m_prev - m_new)
            p = jnp.exp2(s - m_new)
            l_ref[...] = alpha * l_ref[...] + jnp.sum(p, axis=0, keepdims=True)
            acc_ref[...] = alpha * acc_ref[...] + jnp.dot(vt, p.astype(BF16),
                                                          preferred_element_type=F32)
            m_ref[...] = m_new
        else:
            p = jnp.exp2(s - c_ref[0])
            l_ref[...] += jnp.sum(p, axis=0, keepdims=True)
            acc_ref[...] += jnp.dot(vt, p.astype(BF16), preferred_element_type=F32)

    nbmax = ATTN_BLOCKS_PER_TRIP

    def body(jj, carry):
        step(nbmax * jj, nbmax, False)
        return carry

    lax.fori_loop(0, qi // nbmax, body, 0)

    for rem in range(nbmax):
        @pl.when(qi % nbmax == rem)
        def _():
            step(qi - rem, rem + 1, True)

    accn = acc_ref[...] / l_ref[...]
    lp = lam_ref[...]
    lam = (jnp.exp(jnp.sum(lp[0:1, :] * lp[1:2, :], axis=1, keepdims=True))
           - jnp.exp(jnp.sum(lp[2:3, :] * lp[3:4, :], axis=1, keepdims=True)) + lam_init)
    o_t = accn[:, 0:t] - lam * accn[:, t:2 * t]
    ms = jnp.mean(o_t * o_t, axis=0, keepdims=True)
    o = (o_t * lax.rsqrt(ms + NORM_EPS)).T
    out_ref[...] = (o * (gd_ref[...] * (1.0 - lam_init))).astype(BF16)


def _attn(c, P, lam_p, gd, *, lam_init, B, S, online):
    T = B * S
    t = ATTN_T
    nq = S // t
    in_specs = [pl.BlockSpec((t, 128), lambda b, h, i: (b * nq + i, COL_AQ // 128 + h)),
                pl.BlockSpec((S, 128), lambda b, h, i: (b, COL_AK // 128 + h)),
                pl.BlockSpec((S, 128), lambda b, h, i: (b, COL_AV // 128 + h)),
                pl.BlockSpec((4, A_DK), lambda b, h, i: (0, 0)),
                pl.BlockSpec((1, A_DV), lambda b, h, i: (0, 0))]
    scratch = [pltpu.VMEM((128, 2 * t), BF16),
               pltpu.VMEM((128, S), BF16),
               pltpu.VMEM((1, 2 * t), F32),
               pltpu.VMEM((128, 2 * t), F32)]
    args = (P, P, P, lam_p, gd)
    if online:
        scratch = scratch + [pltpu.VMEM((1, 2 * t), F32)]
    else:
        in_specs = [pl.BlockSpec(memory_space=pltpu.SMEM)] + in_specs
        args = (c,) + args
    return pl.pallas_call(
        functools.partial(_attn_kernel, lam_init=lam_init, online=online),
        grid=(B, A_HEADS, nq),
        in_specs=in_specs,
        out_specs=pl.BlockSpec((t, 128), lambda b, h, i: (b * nq + i, h)),
        out_shape=jax.ShapeDtypeStruct((T, A_HEADS * A_DV), BF16),
        scratch_shapes=scratch,
        compiler_params=_params(3),
        name="attn_online" if online else "attn",
    )(*args)


def _merge_kernel(hm_ref, pu_ref, ao_ref, gm_ref, gp_ref, ga_ref, x_ref, wm_ref, wp_ref, ps_ref,
                  wd_ref, wo_ref, bm_ref, bh_ref, out_ref, halo_ref, mrg_ref, win_ref,
                  *, tiles_per_seq):
    tm = x_ref.shape[0]
    G = POOL_G
    i = pl.program_id(0)
    tile_in_seq = i % tiles_per_seq

    @pl.when(tile_in_seq == 0)
    def _():
        halo_ref[...] = jnp.zeros(halo_ref.shape, BF16)

    pos1 = tile_in_seq * tm + lax.broadcasted_iota(jnp.int32, (tm, 1), 0) + 1

    tb = bm_ref.shape[1]
    n_win = len(POOL_WINDOWS)
    for r in range(tm // tb):
        for gi in range(n_win):
            cols = slice(gi * G, (gi + 1) * G)
            prev = halo_ref[:, cols] if r == 0 else pu_ref[r * tb - POOL_HALO:r * tb, cols]
            win = win_ref.at[r * n_win + gi]
            win[...] = jnp.dot(bm_ref[gi], pu_ref[r * tb:(r + 1) * tb, cols],
                               preferred_element_type=F32)
            win[0:POOL_HALO, :] += jnp.dot(bh_ref[gi], prev, preferred_element_type=F32)

    y_m = jnp.dot(hm_ref[...], wm_ref[...], preferred_element_type=F32)
    y_a = jnp.dot(ao_ref[...], wd_ref[...], preferred_element_type=F32)
    mrg_ref[...] = gm_ref[...].astype(F32) * y_m + ga_ref[...].astype(F32) * y_a

    for r in range(tm // tb):
        rows = slice(r * tb, (r + 1) * tb)
        for gi, w in enumerate(POOL_WINDOWS):
            cols = slice(gi * G, (gi + 1) * G)
            cnt = jnp.minimum(pos1[rows], w).astype(F32)
            pooled = win_ref[r * n_win + gi] / cnt - pu_ref[rows, cols].astype(F32)
            y_p = (jnp.dot(pooled.astype(BF16), wp_ref[gi], preferred_element_type=F32)
                   * ps_ref[:, cols])
            mrg_ref[rows, cols] += gp_ref[rows, cols].astype(F32) * y_p
    halo_ref[...] = pu_ref[tm - POOL_HALO:tm, :]

    out_ref[...] = x_ref[...] + jnp.dot(mrg_ref[...].astype(BF16), wo_ref[...],
                                        preferred_element_type=F32)


def _pool_bands(tm):
    t_idx = jnp.arange(tm)[:, None]
    main, halo = [], []
    for w in POOL_WINDOWS:
        d_main = t_idx - jnp.arange(tm)[None, :]
        d_halo = t_idx[:POOL_HALO] - (jnp.arange(POOL_HALO)[None, :] - POOL_HALO)
        main.append((d_main >= 0) & (d_main < w))
        halo.append((d_halo >= 0) & (d_halo < w))
    return jnp.stack(main).astype(BF16), jnp.stack(halo).astype(BF16)


def _merge(hm, P, ao, xf, wm, wp, ps, wd, wo, B, S, l):
    T = B * S
    tm = MERGE_TM
    tb = POOL_BAND_ROWS
    band_main, band_halo = _pool_bands(tb)
    row = lambda w, cb: pl.BlockSpec((tm, w), lambda i: (i, cb))
    full = lambda shape: pl.BlockSpec(shape, lambda i: (0,) * len(shape),
                                      pipeline_mode=pl.Buffered(1))
    layer = lambda shape: pl.BlockSpec((None,) + shape, lambda i: (l,) + (0,) * len(shape),
                                       pipeline_mode=pl.Buffered(1))
    return pl.pallas_call(
        functools.partial(_merge_kernel, tiles_per_seq=S // tm),
        grid=(T // tm,),
        in_specs=[row(D_MODEL, 0), row(D_MODEL, COL_PU // D_MODEL), row(D_MODEL, 0),
                  row(D_MODEL, COL_GATE // D_MODEL), row(D_MODEL, COL_GATE // D_MODEL + 1),
                  row(D_MODEL, COL_GATE // D_MODEL + 2), row(D_MODEL, 0),
                  layer((D_MODEL, D_MODEL)), layer((4, POOL_G, POOL_G)), full((1, D_MODEL)),
                  layer((D_MODEL, D_MODEL)), layer((D_MODEL, D_MODEL)),
                  full((4, tb, tb)), full((4, POOL_HALO, POOL_HALO))],
        out_specs=row(D_MODEL, 0),
        out_shape=jax.ShapeDtypeStruct((T, D_MODEL), F32),
        scratch_shapes=[pltpu.VMEM((POOL_HALO, D_MODEL), BF16),
                        pltpu.VMEM((tm, D_MODEL), F32),
                        pltpu.VMEM((tm // tb * len(POOL_WINDOWS), tb, POOL_G), F32)],
        compiler_params=_params(1),
        name="merge",
    )(hm, P, ao, P, P, P, xf, wm, wp, ps, wd, wo, band_main, band_halo)


def _ffn_kernel(x_ref, g_ref, wgu_ref, wdn_ref, out_ref, act_ref):
    x = x_ref[...]
    ms = jnp.mean(x * x, axis=-1, keepdims=True)
    hb = (x * lax.rsqrt(ms + NORM_EPS) * g_ref[...]).astype(BF16)
    for lo, hi in FFN_CHUNKS:
        gate = jnp.dot(hb, wgu_ref[:, lo:hi], preferred_element_type=F32)
        up = jnp.dot(hb, wgu_ref[:, FF + lo:FF + hi], preferred_element_type=F32)
        act_ref[:, lo:hi] = (gate * _sigmoid(gate) * up).astype(BF16)
    out_ref[...] = x + jnp.dot(act_ref[...], wdn_ref[...], preferred_element_type=F32)


def _ffn(xf, g, wgu, wdn, l):
    T = xf.shape[0]
    tm = FFN_TM
    layer = lambda shape: pl.BlockSpec((None,) + shape, lambda i: (l, 0, 0),
                                       pipeline_mode=pl.Buffered(1))
    return pl.pallas_call(
        _ffn_kernel,
        grid=(T // tm,),
        in_specs=[pl.BlockSpec((tm, D_MODEL), lambda i: (i, 0)),
                  pl.BlockSpec((1, D_MODEL), lambda i: (0, 0)),
                  layer((D_MODEL, 2 * FF)), layer((FF, D_MODEL))],
        out_specs=pl.BlockSpec((tm, D_MODEL), lambda i: (i, 0)),
        out_shape=jax.ShapeDtypeStruct((T, D_MODEL), F32),
        scratch_shapes=[pltpu.VMEM((tm, FF), BF16)],
        compiler_params=_params(1),
        name="ffn",
    )(xf, g, wgu, wdn)


def _rope_tables(S):
    half = A_DK // 2
    inv = ROPE_THETA ** (-jnp.arange(half, dtype=F32) / half)
    ang = jnp.arange(S, dtype=F32)[:, None] * inv[None, :]
    cos, sin = jnp.cos(ang), jnp.sin(ang)
    cos_t = jnp.concatenate([cos, cos, cos, cos], axis=1)
    sin_t = jnp.concatenate([-sin, sin, -sin, sin], axis=1)
    return cos_t, sin_t


def kernel(x, g_mix, w_in, b_if, conv_qk, w_m_out, w_pool, pool_scale, g_qk, lam_p, g_diff_head,
           w_diff_out, w_out, g_ffn, w_gate_up, w_down):
    B, S, _ = x.shape
    T = B * S
    xf = x.reshape(T, D_MODEL)
    cos_t, sin_t = _rope_tables(S)
    lane = jnp.arange(256)
    gmat = (lane[:, None] // A_DK == lane[None, :] // A_DK).astype(BF16)
    n_gate = 2 * M_HEADS
    gate_lo = 4 * M_HEADS * M_DH

    w_main = jnp.concatenate([w_in[:, :, :gate_lo], w_in[:, :, gate_lo + n_gate:]], axis=2).astype(BF16)
    wif_t = jnp.pad(jnp.swapaxes(w_in[:, :, gate_lo:gate_lo + n_gate], 1, 2),
                    ((0, 0), (0, GATE_ROWS - n_gate), (0, 0))).astype(BF16)
    wm, wp, wd, wo = (w.astype(BF16) for w in (w_m_out, w_pool, w_diff_out, w_out))
    wgu, wdn = w_gate_up.astype(BF16), w_down.astype(BF16)

    for l in range(DEPTH):
        bif = jnp.pad(b_if[l], (0, GATE_ROWS - n_gate)).reshape(GATE_ROWS, 1)
        g2 = jnp.concatenate([g_qk[l], g_qk[l]], axis=1)
        lam_init = 0.8 - 0.6 * math.exp(-0.3 * l)

        P, gates = _proj(xf, g_mix[l].reshape(1, D_MODEL), w_main, wif_t, bif, cos_t, sin_t,
                         g2, gmat, S, l)
        hm = _mlstm(P, gates, conv_qk[l], B, S)
        c = (A_DK ** 0.5 * math.log2(math.e)) * jnp.max(jnp.abs(g_qk[l][0])) * jnp.max(jnp.abs(g_qk[l][1]))
        attn_args = (c.reshape(1), P, lam_p[l], g_diff_head[l].reshape(1, A_DV))
        ao = lax.cond(c < ATTN_MAX_STATIC_BOUND,
                      functools.partial(_attn, lam_init=lam_init, B=B, S=S, online=False),
                      functools.partial(_attn, lam_init=lam_init, B=B, S=S, online=True),
                      *attn_args)
        x1 = _merge(hm, P, ao, xf, wm, wp, pool_scale[l].reshape(1, D_MODEL), wd, wo, B, S, l)
        xf = _ffn(x1, g_ffn[l].reshape(1, D_MODEL), wgu, wdn, l)
    return xf.reshape(B, S, D_MODEL)
```

```python
import functools
import math

import jax
import jax.numpy as jnp
from jax import lax
from jax.experimental import pallas as pl
from jax.experimental.pallas import tpu as pltpu

D_MODEL = 1024
DEPTH = 2
M_HEADS = 4
M_DH = 256
M_CHUNK = 128
CONV_W = 4
POOL_WINDOWS = (2, 4, 8, 16)
POOL_G = 256
POOL_HALO = 16
A_HEADS = 8
A_DK = 64
A_DV = 128
FF = 2816
NORM_EPS = 1e-6
NEG_BIG = -1e30
ROPE_THETA = 10000.0

COL_MQ, COL_MK, COL_MV, COL_MO = 0, 1024, 2048, 3072
COL_PU, COL_AQ, COL_AK, COL_AV, COL_GATE = 4096, 5120, 6144, 7168, 8192
N_MAIN = 11264
GATE_ROWS = 16

BF16 = jnp.bfloat16
F32 = jnp.float32

VMEM_LIMIT = 56 * 1024 * 1024

PROJ_TM, PROJ_TN = 1024, 1024
PROJ_CHUNK = 256
MLSTM_CHUNKS_PER_STEP = 4
ATTN_T = 512
ATTN_BLOCKS_PER_TRIP = 4
ATTN_MAX_STATIC_BOUND = 60.0
MERGE_TM = 512
POOL_BAND_ROWS = 256
FFN_TM = 256
FFN_CHUNKS = ((0, 512), (512, 1024), (1024, 1536), (1536, 2048), (2048, 2560), (2560, 2816))


def _params(n_axes):
    return pltpu.CompilerParams(dimension_semantics=("arbitrary",) * n_axes,
                                vmem_limit_bytes=VMEM_LIMIT)


def _sigmoid(x):
    return 1.0 / (1.0 + jnp.exp(-x))


def _proj_kernel(x_ref, g_ref, w_ref, wif_ref, bif_ref, cos_ref, sin_ref, gqk_ref, gm_ref,
                 out_ref, gates_ref, h_ref):
    j = pl.program_id(1)
    tm = x_ref.shape[0]
    C = PROJ_CHUNK
    L, H = M_CHUNK, M_HEADS

    @pl.when(j == 0)
    def _():
        lane = lax.broadcasted_iota(jnp.int32, (GATE_ROWS, L), 1)
        is_input_gate = lax.broadcasted_iota(jnp.int32, (GATE_ROWS, L), 0) < H
        for r in range(tm // C):
            rows = slice(r * C, (r + 1) * C)
            x = x_ref[rows, :]
            ms = jnp.mean(x * x, axis=-1, keepdims=True)
            hb = (x * lax.rsqrt(ms + NORM_EPS) * g_ref[...]).astype(BF16)
            h_ref[rows, :] = hb
            out_ref[rows, :] = jnp.dot(hb, w_ref[...], preferred_element_type=F32).astype(BF16)
            gt = lax.dot_general(wif_ref[...], hb, (((1,), (1,)), ((), ())),
                                 preferred_element_type=F32) + bif_ref[...]
            lf = jnp.minimum(gt, 0.0) - jnp.log(1.0 + jnp.exp(-jnp.abs(gt)))
            for c in range(C // L):
                b = lf[:, c * L:(c + 1) * L]
                for sh in (1, 2, 4, 8, 16, 32, 64):
                    b = b + jnp.where(lane >= sh, pltpu.roll(b, sh, axis=1), 0.0)
                gates_ref[:, r * C + c * L:r * C + (c + 1) * L] = jnp.where(
                    is_input_gate, gt[:, c * L:(c + 1) * L], b)

    def chunk_dot(r):
        return jnp.dot(h_ref[r * C:(r + 1) * C, :], w_ref[...], preferred_element_type=F32)

    is_sig = (j == 3) | (j >= 8)
    is_rope = (j == 5) | (j == 6)

    def chunked(epilogue):
        for r in range(tm // C):
            epilogue(r, chunk_dot(r))

    def store_plain(r, acc):
        out_ref[r * C:(r + 1) * C, :] = acc.astype(BF16)

    def store_sigmoid(r, acc):
        out_ref[r * C:(r + 1) * C, :] = _sigmoid(acc).astype(BF16)

    @pl.when(jnp.logical_not(is_sig | is_rope | (j == 0)))
    def _():
        chunked(store_plain)

    @pl.when(is_sig)
    def _():
        chunked(store_sigmoid)

    @pl.when(is_rope)
    def _():
        g = jnp.where(j == 5, gqk_ref[0:1, :] * (A_DK ** -0.5 * math.log2(math.e)), gqk_ref[1:2, :])
        lane = lax.broadcasted_iota(jnp.int32, (C, 128), 1)
        first_half = (lane & (A_DK // 2)) == 0
        gm = gm_ref[...]

        def store_rope(r, acc):
            cos = cos_ref[r * C:(r + 1) * C, :]
            sin = sin_ref[r * C:(r + 1) * C, :]
            for c2 in range(A_HEADS // 2):
                x2 = acc[:, c2 * 256:(c2 + 1) * 256]
                ss2 = jnp.dot((x2 * x2).astype(BF16), gm, preferred_element_type=F32)
                y2 = x2 * lax.rsqrt(ss2 * (1.0 / A_DK) + NORM_EPS)
                for c in (2 * c2, 2 * c2 + 1):
                    cols = slice(c * 128, (c + 1) * 128)
                    y = y2[:, (c % 2) * 128:(c % 2 + 1) * 128] * g
                    partner = jnp.where(first_half, pltpu.roll(y, 128 - A_DK // 2, axis=1),
                                        pltpu.roll(y, A_DK // 2, axis=1))
                    out_ref[r * C:(r + 1) * C, cols] = (y * cos + partner * sin).astype(BF16)

        chunked(store_rope)


def _proj(xf, g, w_main, wif_t, bif, cos_t, sin_t, g2, gmat, S, l):
    T = xf.shape[0]
    tm, tn = PROJ_TM, PROJ_TN
    tps = S // tm
    const = lambda shape: pl.BlockSpec(shape, lambda i, j: (0, 0))
    return pl.pallas_call(
        _proj_kernel,
        grid=(T // tm, N_MAIN // tn),
        in_specs=[pl.BlockSpec((tm, D_MODEL), lambda i, j: (i, 0)),
                  const((1, D_MODEL)),
                  pl.BlockSpec((None, D_MODEL, tn), lambda i, j: (l, 0, j)),
                  pl.BlockSpec((None, GATE_ROWS, D_MODEL), lambda i, j: (l, 0, 0)),
                  const((GATE_ROWS, 1)),
                  pl.BlockSpec((tm, 128), lambda i, j: (i % tps, 0)),
                  pl.BlockSpec((tm, 128), lambda i, j: (i % tps, 0)),
                  const((2, 128)), const((256, 256))],
        out_specs=[pl.BlockSpec((tm, tn), lambda i, j: (i, j)),
                   pl.BlockSpec((GATE_ROWS, tm), lambda i, j: (0, i))],
        out_shape=[jax.ShapeDtypeStruct((T, N_MAIN), BF16),
                   jax.ShapeDtypeStruct((GATE_ROWS, T), F32)],
        scratch_shapes=[pltpu.VMEM((tm, D_MODEL), BF16)],
        compiler_params=_params(2),
        name="proj",
    )(xf, g, w_main, wif_t, bif, cos_t, sin_t, g2, gmat)


def _mlstm_kernel(q_ref, k_ref, v_ref, o_ref, gt_ref, cw_ref, shift_ref, out_ref, ubuf, c_ref, n_ref,
                  m_ref):
    L, H, Dh = M_CHUNK, M_HEADS, M_DH
    W = H * Dh
    R = q_ref.shape[0]
    NC = R // L
    pairs = [(cc, h) for cc in range(NC) for h in range(H)]

    @pl.when(pl.program_id(1) == 0)
    def _():
        ubuf[0:L, :] = jnp.zeros((L, 2 * W), BF16)
        c_ref[...] = jnp.zeros(c_ref.shape, F32)
        n_ref[...] = jnp.zeros(n_ref.shape, F32)
        m_ref[...] = jnp.full(m_ref.shape, NEG_BIG, F32)

    ubuf[L:L + R, 0:W] = q_ref[...]
    ubuf[L:L + R, W:2 * W] = k_ref[...]

    row = lax.broadcasted_iota(jnp.int32, (L, L), 0)
    col = lax.broadcasted_iota(jnp.int32, (L, L), 1)
    eye = row == col
    tril = row >= col

    def to_col(r):
        return jnp.sum(jnp.where(eye, jnp.broadcast_to(r, (L, L)), 0.0), axis=1, keepdims=True)

    def conv_silu(r0, c0):
        cols = slice(c0, c0 + Dh)
        sh = jnp.dot(shift_ref[...], ubuf[r0:r0 + 2 * L, cols], preferred_element_type=F32)
        y = (cw_ref[3:4, cols] * sh[0:L] + cw_ref[2:3, cols] * sh[L:2 * L]
             + cw_ref[1:2, cols] * sh[2 * L:3 * L] + cw_ref[0:1, cols] * sh[3 * L:4 * L])
        return y / (1.0 + jnp.exp2(y * (-math.log2(math.e))))

    rows_of = lambda cc: slice(cc * L, (cc + 1) * L)
    cols_of = lambda h: slice(h * Dh, (h + 1) * Dh)

    g_tot, rowterm, m_loc, e_col, b_col = {}, {}, {}, {}, {}
    e_row, b_row = {}, {}
    for cc, h in pairs:
        li = gt_ref[h:h + 1, rows_of(cc)]
        b_row[cc, h] = gt_ref[H + h:H + h + 1, rows_of(cc)]
        g_tot[cc, h] = b_row[cc, h][:, L - 1:L]
        rowterm[cc, h] = li - b_row[cc, h]
        w_state = g_tot[cc, h] + rowterm[cc, h]
        m_loc[cc, h] = jnp.max(w_state, axis=1, keepdims=True)
        e_row[cc, h] = jnp.exp(w_state - m_loc[cc, h])
    for cc, h in pairs:
        e_col[cc, h] = to_col(e_row[cc, h])
        b_col[cc, h] = to_col(b_row[cc, h])

    m_prev, a_dec, b_in = {}, {}, {}
    for h in range(H):
        m = m_ref[h:h + 1, 0:1]
        for cc in range(NC):
            m_prev[cc, h] = m
            m_new = jnp.maximum(g_tot[cc, h] + m, m_loc[cc, h])
            a_dec[cc, h] = jnp.exp(g_tot[cc, h] + m - m_new)
            b_in[cc, h] = jnp.exp(m_loc[cc, h] - m_new)
            m = m_new
        m_ref[h:h + 1, :] = jnp.broadcast_to(m, (1, 128))

    qh, kh, qb, kb = {}, {}, {}, {}
    for cc, h in pairs:
        qh[cc, h] = conv_silu(cc * L, h * Dh) * (Dh ** -0.5)
        kh[cc, h] = conv_silu(cc * L, W + h * Dh)
        qb[cc, h] = qh[cc, h].astype(BF16)
        kb[cc, h] = kh[cc, h].astype(BF16)

    raw = {}
    for cc, h in pairs:
        raw[cc, h] = lax.dot_general(qb[cc, h], kb[cc, h], (((1,), (1,)), ((), ())),
                                     preferred_element_type=F32)
    inter, sv, den0, floor = {}, {}, {}, {}
    for cc, h in pairs:
        dm = jnp.where(tril, b_col[cc, h] + rowterm[cc, h], NEG_BIG)
        m_intra = jnp.max(dm, axis=1, keepdims=True)
        m_inter = b_col[cc, h] + m_prev[cc, h]
        m_t = jnp.maximum(m_inter, m_intra)
        sqk = raw[cc, h] * jnp.exp(dm - m_t)
        inter[cc, h] = jnp.exp(m_inter - m_t)
        floor[cc, h] = jnp.exp(-m_t)
        den0[cc, h] = jnp.sum(sqk, axis=1, keepdims=True)
        sv[cc, h] = jnp.dot(sqk.astype(BF16), v_ref[rows_of(cc), cols_of(h)],
                            preferred_element_type=F32)

    for cc in range(NC):
        c_loc, n_loc = {}, {}
        for h in range(H):
            ke = kh[cc, h] * e_col[cc, h]
            c_loc[h] = lax.dot_general(ke.astype(BF16), v_ref[rows_of(cc), cols_of(h)],
                                       (((0,), (0,)), ((), ())), preferred_element_type=F32)
            n_loc[h] = jnp.sum(ke, axis=0, keepdims=True)
        for h in range(H):
            c_prev = c_ref[h]
            n_prev = n_ref[h:h + 1, :]
            q_c = jnp.dot(qb[cc, h], c_prev.astype(BF16), preferred_element_type=F32)
            q_n = jnp.sum(qh[cc, h] * n_prev, axis=1, keepdims=True)
            num = sv[cc, h] + inter[cc, h] * q_c
            den = den0[cc, h] + inter[cc, h] * q_n
            hh = num / jnp.maximum(jnp.abs(den), floor[cc, h])
            out_ref[rows_of(cc), cols_of(h)] = (
                o_ref[rows_of(cc), cols_of(h)].astype(F32) * hh).astype(BF16)
            c_ref[h] = a_dec[cc, h] * c_prev + b_in[cc, h] * c_loc[h]
            n_ref[h:h + 1, :] = a_dec[cc, h] * n_prev + b_in[cc, h] * n_loc[h]

    ubuf[0:L, :] = ubuf[R:R + L, :]


def _mlstm(P, gates, conv_w, B, S):
    T = B * S
    L, H, Dh = M_CHUNK, M_HEADS, M_DH
    R = MLSTM_CHUNKS_PER_STEP * L
    ns = S // R
    W = H * Dh
    blk = lambda cb: pl.BlockSpec((R, W), lambda b, c: (b * ns + c, cb))
    t_idx = jnp.arange(CONV_W * L) % L
    s_idx = jnp.arange(CONV_W * L) // L
    shift = (jnp.arange(2 * L)[None, :] == (L + t_idx - s_idx)[:, None]).astype(BF16)
    return pl.pallas_call(
        _mlstm_kernel,
        grid=(B, ns),
        in_specs=[blk(COL_MQ // W), blk(COL_MK // W), blk(COL_MV // W), blk(COL_MO // W),
                  pl.BlockSpec((GATE_ROWS, R), lambda b, c: (0, b * ns + c)),
                  pl.BlockSpec((CONV_W, 2 * W), lambda b, c: (0, 0)),
                  pl.BlockSpec((CONV_W * L, 2 * L), lambda b, c: (0, 0))],
        out_specs=pl.BlockSpec((R, W), lambda b, c: (b * ns + c, 0)),
        out_shape=jax.ShapeDtypeStruct((T, W), BF16),
        scratch_shapes=[pltpu.VMEM((L + R, 2 * W), BF16),
                        pltpu.VMEM((H, Dh, Dh), F32),
                        pltpu.VMEM((8, Dh), F32),
                        pltpu.VMEM((8, 128), F32)],
        compiler_params=_params(2),
        name="mlstm",
    )(P, P, P, P, gates, conv_w, shift)


def _attn_kernel(*refs, lam_init, online):
    if online:
        q_ref, k_ref, v_ref, lam_ref, gd_ref, out_ref, qs_ref, vt_ref, l_ref, acc_ref, m_ref = refs
    else:
        c_ref, q_ref, k_ref, v_ref, lam_ref, gd_ref, out_ref, qs_ref, vt_ref, l_ref, acc_ref = refs
    t = ATTN_T
    nq = q_ref.shape[0] // t
    nbmax = ATTN_BLOCKS_PER_TRIP

    lp = lam_ref[...]
    lam = (jnp.exp(jnp.sum(lp[0:1, :] * lp[1:2, :], axis=1, keepdims=True))
           - jnp.exp(jnp.sum(lp[2:3, :] * lp[3:4, :], axis=1, keepdims=True)) + lam_init)
    gd = gd_ref[...] * (1.0 - lam_init)

    for c in range(nq):
        vt_ref[:, c * t:(c + 1) * t] = v_ref[c * t:(c + 1) * t, :].astype(F32).T.astype(BF16)

    def step(qi, j, nb, diag_last):
        k = k_ref[j * t:(j + nb) * t, :]
        vt = vt_ref[:, j * t:(j + nb) * t]
        s = jnp.dot(k, qs_ref[qi], preferred_element_type=F32)
        if diag_last:
            row = lax.broadcasted_iota(jnp.int32, (nb * t, 2 * t), 0) - (nb - 1) * t
            col = lax.broadcasted_iota(jnp.int32, (nb * t, 2 * t), 1)
            qcol = jnp.where(col >= t, col - t, col)
            s = jnp.where(row <= qcol, s, NEG_BIG)
        if online:
            m_prev = m_ref[qi]
            m_new = jnp.maximum(m_prev, jnp.max(s, axis=0, keepdims=True))
            alpha = jnp.exp2(m_prev - m_new)
            p = jnp.exp2(s - m_new)
            l_ref[qi] = alpha * l_ref[qi] + jnp.sum(p, axis=0, keepdims=True)
            acc_ref[qi] = alpha * acc_ref[qi] + jnp.dot(vt, p.astype(BF16),
                                                        preferred_element_type=F32)
            m_ref[qi] = m_new
        else:
            p = jnp.exp2(s - c_ref[0])
            l_ref[qi] += jnp.sum(p, axis=0, keepdims=True)
            acc_ref[qi] += jnp.dot(vt, p.astype(BF16), preferred_element_type=F32)

    for qi in range(nq):
        q_t = q_ref[qi * t:(qi + 1) * t, :].astype(F32).T
        sub = lax.broadcasted_iota(jnp.int32, (128, t), 0)
        qs_ref[qi, :, 0:t] = jnp.where(sub < A_DK, q_t, 0.0).astype(BF16)
        qs_ref[qi, :, t:2 * t] = jnp.where(sub >= A_DK, q_t, 0.0).astype(BF16)
        l_ref[qi] = jnp.zeros((1, 2 * t), F32)
        acc_ref[qi] = jnp.zeros((128, 2 * t), F32)
        if online:
            m_ref[qi] = jnp.full((1, 2 * t), NEG_BIG, F32)

        j = 0
        while qi + 1 - j > nbmax:
            step(qi, j, nbmax, False)
            j += nbmax
        step(qi, j, qi + 1 - j, True)

        accn = acc_ref[qi] / l_ref[qi]
        o_t = accn[:, 0:t] - lam * accn[:, t:2 * t]
        ms = jnp.mean(o_t * o_t, axis=0, keepdims=True)
        o = (o_t * lax.rsqrt(ms + NORM_EPS)).T
        out_ref[qi * t:(qi + 1) * t, :] = (o * gd).astype(BF16)


def _attn(c, P, lam_p, gd, *, lam_init, B, S, online):
    T = B * S
    t = ATTN_T
    nq = S // t
    in_specs = [pl.BlockSpec((S, 128), lambda b, h: (b, COL_AQ // 128 + h)),
                pl.BlockSpec((S, 128), lambda b, h: (b, COL_AK // 128 + h)),
                pl.BlockSpec((S, 128), lambda b, h: (b, COL_AV // 128 + h)),
                pl.BlockSpec((4, A_DK), lambda b, h: (0, 0)),
                pl.BlockSpec((1, A_DV), lambda b, h: (0, 0))]
    scratch = [pltpu.VMEM((nq, 128, 2 * t), BF16),
               pltpu.VMEM((128, S), BF16),
               pltpu.VMEM((nq, 1, 2 * t), F32),
               pltpu.VMEM((nq, 128, 2 * t), F32)]
    args = (P, P, P, lam_p, gd)
    if online:
        scratch = scratch + [pltpu.VMEM((nq, 1, 2 * t), F32)]
    else:
        in_specs = [pl.BlockSpec(memory_space=pltpu.SMEM)] + in_specs
        args = (c,) + args
    return pl.pallas_call(
        functools.partial(_attn_kernel, lam_init=lam_init, online=online),
        grid=(B, A_HEADS),
        in_specs=in_specs,
        out_specs=pl.BlockSpec((S, 128), lambda b, h: (b, h)),
        out_shape=jax.ShapeDtypeStruct((T, A_HEADS * A_DV), BF16),
        scratch_shapes=scratch,
        compiler_params=_params(2),
        name="attn_online" if online else "attn",
    )(*args)


def _merge_kernel(hm_ref, pu_ref, ao_ref, gm_ref, gp_ref, ga_ref, x_ref, wm_ref, wp_ref, ps_ref,
                  wd_ref, wo_ref, bm_ref, bh_ref, out_ref, halo_ref, mrg_ref, win_ref,
                  *, tiles_per_seq):
    tm = x_ref.shape[0]
    G = POOL_G
    i = pl.program_id(0)
    tile_in_seq = i % tiles_per_seq

    @pl.when(tile_in_seq == 0)
    def _():
        halo_ref[...] = jnp.zeros(halo_ref.shape, BF16)

    pos1 = tile_in_seq * tm + lax.broadcasted_iota(jnp.int32, (tm, 1), 0) + 1

    tb = bm_ref.shape[1]
    n_win = len(POOL_WINDOWS)
    for r in range(tm // tb):
        for gi in range(n_win):
            cols = slice(gi * G, (gi + 1) * G)
            prev = halo_ref[:, cols] if r == 0 else pu_ref[r * tb - POOL_HALO:r * tb, cols]
            win = win_ref.at[r * n_win + gi]
            win[...] = jnp.dot(bm_ref[gi], pu_ref[r * tb:(r + 1) * tb, cols],
                               preferred_element_type=F32)
            win[0:POOL_HALO, :] += jnp.dot(bh_ref[gi], prev, preferred_element_type=F32)

    y_m = jnp.dot(hm_ref[...], wm_ref[...], preferred_element_type=F32)
    y_a = jnp.dot(ao_ref[...], wd_ref[...], preferred_element_type=F32)
    mrg_ref[...] = gm_ref[...].astype(F32) * y_m + ga_ref[...].astype(F32) * y_a

    for r in range(tm // tb):
        rows = slice(r * tb, (r + 1) * tb)
        for gi, w in enumerate(POOL_WINDOWS):
            cols = slice(gi * G, (gi + 1) * G)
            cnt = jnp.minimum(pos1[rows], w).astype(F32)
            pooled = win_ref[r * n_win + gi] / cnt - pu_ref[rows, cols].astype(F32)
            y_p = (jnp.dot(pooled.astype(BF16), wp_ref[gi], preferred_element_type=F32)
                   * ps_ref[:, cols])
            mrg_ref[rows, cols] += gp_ref[rows, cols].astype(F32) * y_p
    halo_ref[...] = pu_ref[tm - POOL_HALO:tm, :]

    out_ref[...] = x_ref[...] + jnp.dot(mrg_ref[...].astype(BF16), wo_ref[...],
                                        preferred_element_type=F32)


def _pool_bands(tm):
    t_idx = jnp.arange(tm)[:, None]
    main, halo = [], []
    for w in POOL_WINDOWS:
        d_main = t_idx - jnp.arange(tm)[None, :]
        d_halo = t_idx[:POOL_HALO] - (jnp.arange(POOL_HALO)[None, :] - POOL_HALO)
        main.append((d_main >= 0) & (d_main < w))
        halo.append((d_halo >= 0) & (d_halo < w))
    return jnp.stack(main).astype(BF16), jnp.stack(halo).astype(BF16)


def _merge(hm, P, ao, xf, wm, wp, ps, wd, wo, B, S, l):
    T = B * S
    tm = MERGE_TM
    tb = POOL_BAND_ROWS
    band_main, band_halo = _pool_bands(tb)
    row = lambda w, cb: pl.BlockSpec((tm, w), lambda i: (i, cb))
    full = lambda shape: pl.BlockSpec(shape, lambda i: (0,) * len(shape),
                                      pipeline_mode=pl.Buffered(1))
    layer = lambda shape: pl.BlockSpec((None,) + shape, lambda i: (l,) + (0,) * len(shape),
                                       pipeline_mode=pl.Buffered(1))
    return pl.pallas_call(
        functools.partial(_merge_kernel, tiles_per_seq=S // tm),
        grid=(T // tm,),
        in_specs=[row(D_MODEL, 0), row(D_MODEL, COL_PU // D_MODEL), row(D_MODEL, 0),
                  row(D_MODEL, COL_GATE // D_MODEL), row(D_MODEL, COL_GATE // D_MODEL + 1),
                  row(D_MODEL, COL_GATE // D_MODEL + 2), row(D_MODEL, 0),
                  layer((D_MODEL, D_MODEL)), layer((4, POOL_G, POOL_G)), full((1, D_MODEL)),
                  layer((D_MODEL, D_MODEL)), layer((D_MODEL, D_MODEL)),
                  full((4, tb, tb)), full((4, POOL_HALO, POOL_HALO))],
        out_specs=row(D_MODEL, 0),
        out_shape=jax.ShapeDtypeStruct((T, D_MODEL), F32),
        scratch_shapes=[pltpu.VMEM((POOL_HALO, D_MODEL), BF16),
                        pltpu.VMEM((tm, D_MODEL), F32),
                        pltpu.VMEM((tm // tb * len(POOL_WINDOWS), tb, POOL_G), F32)],
        compiler_params=_params(1),
        name="merge",
    )(hm, P, ao, P, P, P, xf, wm, wp, ps, wd, wo, band_main, band_halo)


def _ffn_kernel(x_ref, g_ref, wgu_ref, wdn_ref, out_ref, act_ref):
    x = x_ref[...]
    ms = jnp.mean(x * x, axis=-1, keepdims=True)
    hb = (x * lax.rsqrt(ms + NORM_EPS) * g_ref[...]).astype(BF16)
    for lo, hi in FFN_CHUNKS:
        gate = jnp.dot(hb, wgu_ref[:, lo:hi], preferred_element_type=F32)
        up = jnp.dot(hb, wgu_ref[:, FF + lo:FF + hi], preferred_element_type=F32)
        act_ref[:, lo:hi] = (gate * _sigmoid(gate) * up).astype(BF16)
    out_ref[...] = x + jnp.dot(act_ref[...], wdn_ref[...], preferred_element_type=F32)


def _ffn(xf, g, wgu, wdn, l):
    T = xf.shape[0]
    tm = FFN_TM
    layer = lambda shape: pl.BlockSpec((None,) + shape, lambda i: (l, 0, 0),
                                       pipeline_mode=pl.Buffered(1))
    return pl.pallas_call(
        _ffn_kernel,
        grid=(T // tm,),
        in_specs=[pl.BlockSpec((tm, D_MODEL), lambda i: (i, 0)),
                  pl.BlockSpec((1, D_MODEL), lambda i: (0, 0)),
                  layer((D_MODEL, 2 * FF)), layer((FF, D_MODEL))],
        out_specs=pl.BlockSpec((tm, D_MODEL), lambda i: (i, 0)),
        out_shape=jax.ShapeDtypeStruct((T, D_MODEL), F32),
        scratch_shapes=[pltpu.VMEM((tm, FF), BF16)],
        compiler_params=_params(1),
        name="ffn",
    )(xf, g, wgu, wdn)


def _rope_tables(S):
    half = A_DK // 2
    inv = ROPE_THETA ** (-jnp.arange(half, dtype=F32) / half)
    ang = jnp.arange(S, dtype=F32)[:, None] * inv[None, :]
    cos, sin = jnp.cos(ang), jnp.sin(ang)
    cos_t = jnp.concatenate([cos, cos, cos, cos], axis=1)
    sin_t = jnp.concatenate([-sin, sin, -sin, sin], axis=1)
    return cos_t, sin_t


def kernel(x, g_mix, w_in, b_if, conv_qk, w_m_out, w_pool, pool_scale, g_qk, lam_p, g_diff_head,
           w_diff_out, w_out, g_ffn, w_gate_up, w_down):
    B, S, _ = x.shape
    T = B * S
    xf = x.reshape(T, D_MODEL)
    cos_t, sin_t = _rope_tables(S)
    lane = jnp.arange(256)
    gmat = (lane[:, None] // A_DK == lane[None, :] // A_DK).astype(BF16)
    n_gate = 2 * M_HEADS
    gate_lo = 4 * M_HEADS * M_DH

    w_main = jnp.concatenate([w_in[:, :, :gate_lo], w_in[:, :, gate_lo + n_gate:]], axis=2).astype(BF16)
    wif_t = jnp.pad(jnp.swapaxes(w_in[:, :, gate_lo:gate_lo + n_gate], 1, 2),
                    ((0, 0), (0, GATE_ROWS - n_gate), (0, 0))).astype(BF16)
    wm, wp, wd, wo = (w.astype(BF16) for w in (w_m_out, w_pool, w_diff_out, w_out))
    wgu, wdn = w_gate_up.astype(BF16), w_down.astype(BF16)

    for l in range(DEPTH):
        bif = jnp.pad(b_if[l], (0, GATE_ROWS - n_gate)).reshape(GATE_ROWS, 1)
        g2 = jnp.concatenate([g_qk[l], g_qk[l]], axis=1)
        lam_init = 0.8 - 0.6 * math.exp(-0.3 * l)

        P, gates = _proj(xf, g_mix[l].reshape(1, D_MODEL), w_main, wif_t, bif, cos_t, sin_t,
                         g2, gmat, S, l)
        hm = _mlstm(P, gates, conv_qk[l], B, S)
        c = (A_DK ** 0.5 * math.log2(math.e)) * jnp.max(jnp.abs(g_qk[l][0])) * jnp.max(jnp.abs(g_qk[l][1]))
        attn_args = (c.reshape(1), P, lam_p[l], g_diff_head[l].reshape(1, A_DV))
        ao = lax.cond(c < ATTN_MAX_STATIC_BOUND,
                      functools.partial(_attn, lam_init=lam_init, B=B, S=S, online=False),
                      functools.partial(_attn, lam_init=lam_init, B=B, S=S, online=True),
                      *attn_args)
        x1 = _merge(hm, P, ao, xf, wm, wp, pool_scale[l].reshape(1, D_MODEL), wd, wo, B, S, l)
        xf = _ffn(x1, g_ffn[l].reshape(1, D_MODEL), wgu, wdn, l)
    return xf.reshape(B, S, D_MODEL)
```

```python
import functools
import math

import jax
import jax.numpy as jnp
from jax import lax
from jax.experimental import pallas as pl
from jax.experimental.pallas import tpu as pltpu

D_MODEL = 1024
DEPTH = 2
M_HEADS = 4
M_DH = 256
M_CHUNK = 128
CONV_W = 4
POOL_WINDOWS = (2, 4, 8, 16)
POOL_G = 256
POOL_HALO = 16
A_HEADS = 8
A_DK = 64
A_DV = 128
FF = 2816
NORM_EPS = 1e-6
NEG_BIG = -1e30
ROPE_THETA = 10000.0

COL_MQ, COL_MK, COL_MV, COL_MO = 0, 1024, 2048, 3072
COL_PU, COL_AQ, COL_AK, COL_AV, COL_GATE = 4096, 5120, 6144, 7168, 8192
N_MAIN = 11264
GATE_ROWS = 16

BF16 = jnp.bfloat16
F32 = jnp.float32

VMEM_LIMIT = 56 * 1024 * 1024

PROJ_TM, PROJ_TN = 256, 1024
MLSTM_CHUNKS_PER_STEP = 4
ATTN_T = 512
ATTN_BLOCKS_PER_TRIP = 4
ATTN_MAX_STATIC_BOUND = 60.0
MERGE_TM = 512
POOL_BAND_ROWS = 256
FFN_TM = 256
FFN_CHUNKS = ((0, 512), (512, 1024), (1024, 1536), (1536, 2048), (2048, 2560), (2560, 2816))


def _params(n_axes):
    return pltpu.CompilerParams(dimension_semantics=("arbitrary",) * n_axes,
                                vmem_limit_bytes=VMEM_LIMIT)


def _sigmoid(x):
    return 1.0 / (1.0 + jnp.exp(-x))


def _proj_kernel(x_ref, g_ref, wa_ref, wb_ref, wif_ref, bif_ref, cos_ref, sin_ref, gqk_ref, gm_ref,
                 out_ref, gates_ref):
    tm = x_ref.shape[0]
    tn = PROJ_TN
    L, H = M_CHUNK, M_HEADS

    x = x_ref[...]
    ms = jnp.mean(x * x, axis=-1, keepdims=True)
    hb = (x * lax.rsqrt(ms + NORM_EPS) * g_ref[...]).astype(BF16)

    gt = lax.dot_general(wif_ref[...], hb, (((1,), (1,)), ((), ())),
                         preferred_element_type=F32) + bif_ref[...]
    lf = jnp.minimum(gt, 0.0) - jnp.log(1.0 + jnp.exp(-jnp.abs(gt)))
    lane = lax.broadcasted_iota(jnp.int32, (GATE_ROWS, L), 1)
    is_input_gate = lax.broadcasted_iota(jnp.int32, (GATE_ROWS, L), 0) < H
    for c in range(tm // L):
        b = lf[:, c * L:(c + 1) * L]
        for sh in (1, 2, 4, 8, 16, 32, 64):
            b = b + jnp.where(lane >= sh, pltpu.roll(b, sh, axis=1), 0.0)
        gates_ref[:, c * L:(c + 1) * L] = jnp.where(is_input_gate, gt[:, c * L:(c + 1) * L], b)

    lane_r = lax.broadcasted_iota(jnp.int32, (tm, 128), 1)
    first_half = (lane_r & (A_DK // 2)) == 0
    cos = cos_ref[...]
    sin = sin_ref[...]

    n_a = wa_ref.shape[1] // tn
    for j in range(N_MAIN // tn):
        w_tile = wa_ref[:, j * tn:(j + 1) * tn] if j < n_a else wb_ref[:, (j - n_a) * tn:(j - n_a + 1) * tn]
        acc = jnp.dot(hb, w_tile, preferred_element_type=F32)
        c0 = j * tn
        if c0 in (COL_MO, COL_GATE, COL_GATE + tn, COL_GATE + 2 * tn):
            out_ref[:, c0:c0 + tn] = _sigmoid(acc).astype(BF16)
        elif c0 in (COL_AQ, COL_AK):
            g = (gqk_ref[0:1, :] * (A_DK ** -0.5 * math.log2(math.e)) if c0 == COL_AQ
                 else gqk_ref[1:2, :])
            for c2 in range(A_HEADS // 2):
                x2 = acc[:, c2 * 256:(c2 + 1) * 256]
                ss2 = jnp.dot((x2 * x2).astype(BF16), gm_ref[...],
                              preferred_element_type=F32)
                y2 = x2 * lax.rsqrt(ss2 * (1.0 / A_DK) + NORM_EPS)
                for c in (2 * c2, 2 * c2 + 1):
                    y = y2[:, (c % 2) * 128:(c % 2 + 1) * 128] * g
                    partner = jnp.where(first_half, pltpu.roll(y, 128 - A_DK // 2, axis=1),
                                        pltpu.roll(y, A_DK // 2, axis=1))
                    out_ref[:, c0 + c * 128:c0 + (c + 1) * 128] = (
                        y * cos + partner * sin).astype(BF16)
        else:
            out_ref[:, c0:c0 + tn] = acc.astype(BF16)


def _proj(xf, g, w_a, w_b, wif_t, bif, cos_t, sin_t, g2, gmat, S, l):
    T = xf.shape[0]
    tm = PROJ_TM
    tps = S // tm
    const = lambda shape: pl.BlockSpec(shape, lambda i: (0, 0))
    layer = lambda shape: pl.BlockSpec((None,) + shape, lambda i: (l, 0, 0),
                                       pipeline_mode=pl.Buffered(1))
    return pl.pallas_call(
        _proj_kernel,
        grid=(T // tm,),
        in_specs=[pl.BlockSpec((tm, D_MODEL), lambda i: (i, 0)),
                  const((1, D_MODEL)),
                  layer((D_MODEL, w_a.shape[2])), layer((D_MODEL, w_b.shape[2])),
                  layer((GATE_ROWS, D_MODEL)),
                  const((GATE_ROWS, 1)),
                  pl.BlockSpec((tm, 128), lambda i: (i % tps, 0)),
                  pl.BlockSpec((tm, 128), lambda i: (i % tps, 0)),
                  const((2, 128)), const((256, 256))],
        out_specs=[pl.BlockSpec((tm, N_MAIN), lambda i: (i, 0)),
                   pl.BlockSpec((GATE_ROWS, tm), lambda i: (0, i))],
        out_shape=[jax.ShapeDtypeStruct((T, N_MAIN), BF16),
                   jax.ShapeDtypeStruct((GATE_ROWS, T), F32)],
        compiler_params=_params(1),
        name="proj",
    )(xf, g, w_a, w_b, wif_t, bif, cos_t, sin_t, g2, gmat)


def _mlstm_kernel(q_ref, k_ref, v_ref, o_ref, gt_ref, cw_ref, shift_ref, out_ref, ubuf, c_ref, n_ref,
                  m_ref):
    L, H, Dh = M_CHUNK, M_HEADS, M_DH
    W = H * Dh
    R = q_ref.shape[0]
    NC = R // L
    pairs = [(cc, h) for cc in range(NC) for h in range(H)]

    @pl.when(pl.program_id(1) == 0)
    def _():
        ubuf[0:L, :] = jnp.zeros((L, 2 * W), BF16)
        c_ref[...] = jnp.zeros(c_ref.shape, F32)
        n_ref[...] = jnp.zeros(n_ref.shape, F32)
        m_ref[...] = jnp.full(m_ref.shape, NEG_BIG, F32)

    ubuf[L:L + R, 0:W] = q_ref[...]
    ubuf[L:L + R, W:2 * W] = k_ref[...]

    row = lax.broadcasted_iota(jnp.int32, (L, L), 0)
    col = lax.broadcasted_iota(jnp.int32, (L, L), 1)
    eye = row == col
    tril = row >= col

    def to_col(r):
        return jnp.sum(jnp.where(eye, jnp.broadcast_to(r, (L, L)), 0.0), axis=1, keepdims=True)

    def conv_silu(r0, c0):
        cols = slice(c0, c0 + Dh)
        sh = jnp.dot(shift_ref[...], ubuf[r0:r0 + 2 * L, cols], preferred_element_type=F32)
        y = (cw_ref[3:4, cols] * sh[0:L] + cw_ref[2:3, cols] * sh[L:2 * L]
             + cw_ref[1:2, cols] * sh[2 * L:3 * L] + cw_ref[0:1, cols] * sh[3 * L:4 * L])
        return y / (1.0 + jnp.exp2(y * (-math.log2(math.e))))

    rows_of = lambda cc: slice(cc * L, (cc + 1) * L)
    cols_of = lambda h: slice(h * Dh, (h + 1) * Dh)

    g_tot, rowterm, m_loc, e_col, b_col = {}, {}, {}, {}, {}
    e_row, b_row = {}, {}
    for cc, h in pairs:
        li = gt_ref[h:h + 1, rows_of(cc)]
        b_row[cc, h] = gt_ref[H + h:H + h + 1, rows_of(cc)]
        g_tot[cc, h] = b_row[cc, h][:, L - 1:L]
        rowterm[cc, h] = li - b_row[cc, h]
        w_state = g_tot[cc, h] + rowterm[cc, h]
        m_loc[cc, h] = jnp.max(w_state, axis=1, keepdims=True)
        e_row[cc, h] = jnp.exp(w_state - m_loc[cc, h])
    for cc, h in pairs:
        e_col[cc, h] = to_col(e_row[cc, h])
        b_col[cc, h] = to_col(b_row[cc, h])

    m_prev, a_dec, b_in = {}, {}, {}
    for h in range(H):
        m = m_ref[h:h + 1, 0:1]
        for cc in range(NC):
            m_prev[cc, h] = m
            m_new = jnp.maximum(g_tot[cc, h] + m, m_loc[cc, h])
            a_dec[cc, h] = jnp.exp(g_tot[cc, h] + m - m_new)
            b_in[cc, h] = jnp.exp(m_loc[cc, h] - m_new)
            m = m_new
        m_ref[h:h + 1, :] = jnp.broadcast_to(m, (1, 128))

    qh, kh, qb, kb = {}, {}, {}, {}
    for cc, h in pairs:
        qh[cc, h] = conv_silu(cc * L, h * Dh) * (Dh ** -0.5)
        kh[cc, h] = conv_silu(cc * L, W + h * Dh)
        qb[cc, h] = qh[cc, h].astype(BF16)
        kb[cc, h] = kh[cc, h].astype(BF16)

    raw = {}
    for cc, h in pairs:
        raw[cc, h] = lax.dot_general(qb[cc, h], kb[cc, h], (((1,), (1,)), ((), ())),
                                     preferred_element_type=F32)
    inter, sv, den0, floor = {}, {}, {}, {}
    for cc, h in pairs:
        dm = jnp.where(tril, b_col[cc, h] + rowterm[cc, h], NEG_BIG)
        m_intra = jnp.max(dm, axis=1, keepdims=True)
        m_inter = b_col[cc, h] + m_prev[cc, h]
        m_t = jnp.maximum(m_inter, m_intra)
        sqk = raw[cc, h] * jnp.exp(dm - m_t)
        inter[cc, h] = jnp.exp(m_inter - m_t)
        floor[cc, h] = jnp.exp(-m_t)
        den0[cc, h] = jnp.sum(sqk, axis=1, keepdims=True)
        sv[cc, h] = jnp.dot(sqk.astype(BF16), v_ref[rows_of(cc), cols_of(h)],
                            preferred_element_type=F32)

    for cc in range(NC):
        c_loc, n_loc = {}, {}
        for h in range(H):
            ke = kh[cc, h] * e_col[cc, h]
            c_loc[h] = lax.dot_general(ke.astype(BF16), v_ref[rows_of(cc), cols_of(h)],
                                       (((0,), (0,)), ((), ())), preferred_element_type=F32)
            n_loc[h] = jnp.sum(ke, axis=0, keepdims=True)
        for h in range(H):
            c_prev = c_ref[h]
            n_prev = n_ref[h:h + 1, :]
            q_c = jnp.dot(qb[cc, h], c_prev.astype(BF16), preferred_element_type=F32)
            q_n = jnp.sum(qh[cc, h] * n_prev, axis=1, keepdims=True)
            num = sv[cc, h] + inter[cc, h] * q_c
            den = den0[cc, h] + inter[cc, h] * q_n
            hh = num / jnp.maximum(jnp.abs(den), floor[cc, h])
            out_ref[rows_of(cc), cols_of(h)] = (
                o_ref[rows_of(cc), cols_of(h)].astype(F32) * hh).astype(BF16)
            c_ref[h] = a_dec[cc, h] * c_prev + b_in[cc, h] * c_loc[h]
            n_ref[h:h + 1, :] = a_dec[cc, h] * n_prev + b_in[cc, h] * n_loc[h]

    ubuf[0:L, :] = ubuf[R:R + L, :]


def _mlstm(P, gates, conv_w, B, S):
    T = B * S
    L, H, Dh = M_CHUNK, M_HEADS, M_DH
    R = MLSTM_CHUNKS_PER_STEP * L
    ns = S // R
    W = H * Dh
    blk = lambda cb: pl.BlockSpec((R, W), lambda b, c: (b * ns + c, cb))
    t_idx = jnp.arange(CONV_W * L) % L
    s_idx = jnp.arange(CONV_W * L) // L
    shift = (jnp.arange(2 * L)[None, :] == (L + t_idx - s_idx)[:, None]).astype(BF16)
    return pl.pallas_call(
        _mlstm_kernel,
        grid=(B, ns),
        in_specs=[blk(COL_MQ // W), blk(COL_MK // W), blk(COL_MV // W), blk(COL_MO // W),
                  pl.BlockSpec((GATE_ROWS, R), lambda b, c: (0, b * ns + c)),
                  pl.BlockSpec((CONV_W, 2 * W), lambda b, c: (0, 0)),
                  pl.BlockSpec((CONV_W * L, 2 * L), lambda b, c: (0, 0))],
        out_specs=pl.BlockSpec((R, W), lambda b, c: (b * ns + c, 0)),
        out_shape=jax.ShapeDtypeStruct((T, W), BF16),
        scratch_shapes=[pltpu.VMEM((L + R, 2 * W), BF16),
                        pltpu.VMEM((H, Dh, Dh), F32),
                        pltpu.VMEM((8, Dh), F32),
                        pltpu.VMEM((8, 128), F32)],
        compiler_params=_params(2),
        name="mlstm",
    )(P, P, P, P, gates, conv_w, shift)


def _attn_kernel(*refs, lam_init, online):
    if online:
        q_ref, k_ref, v_ref, lam_ref, gd_ref, out_ref, qs_ref, vt_ref, l_ref, acc_ref, m_ref = refs
    else:
        c_ref, q_ref, k_ref, v_ref, lam_ref, gd_ref, out_ref, qs_ref, vt_ref, l_ref, acc_ref = refs
    t = ATTN_T
    nq = q_ref.shape[0] // t
    nbmax = ATTN_BLOCKS_PER_TRIP

    lp = lam_ref[...]
    lam = (jnp.exp(jnp.sum(lp[0:1, :] * lp[1:2, :], axis=1, keepdims=True))
           - jnp.exp(jnp.sum(lp[2:3, :] * lp[3:4, :], axis=1, keepdims=True)) + lam_init)
    gd = gd_ref[...] * (1.0 - lam_init)

    for c in range(nq):
        vt_ref[:, c * t:(c + 1) * t] = v_ref[c * t:(c + 1) * t, :].astype(F32).T.astype(BF16)

    def step(qi, j, nb, diag_last):
        k = k_ref[j * t:(j + nb) * t, :]
        vt = vt_ref[:, j * t:(j + nb) * t]
        s = jnp.dot(k, qs_ref[qi], preferred_element_type=F32)
        if diag_last:
            row = lax.broadcasted_iota(jnp.int32, (nb * t, 2 * t), 0) - (nb - 1) * t
            col = lax.broadcasted_iota(jnp.int32, (nb * t, 2 * t), 1)
            qcol = jnp.where(col >= t, col - t, col)
            s = jnp.where(row <= qcol, s, NEG_BIG)
        if online:
            m_prev = m_ref[qi]
            m_new = jnp.maximum(m_prev, jnp.max(s, axis=0, keepdims=True))
            alpha = jnp.exp2(m_prev - m_new)
            p = jnp.exp2(s - m_new)
            l_ref[qi] = alpha * l_ref[qi] + jnp.sum(p, axis=0, keepdims=True)
            acc_ref[qi] = alpha * acc_ref[qi] + jnp.dot(vt, p.astype(BF16),
                                                        preferred_element_type=F32)
            m_ref[qi] = m_new
        else:
            p = jnp.exp2(s - c_ref[0])
            l_ref[qi] += jnp.sum(p, axis=0, keepdims=True)
            acc_ref[qi] += jnp.dot(vt, p.astype(BF16), preferred_element_type=F32)

    for qi in range(nq):
        q_t = q_ref[qi * t:(qi + 1) * t, :].astype(F32).T
        sub = lax.broadcasted_iota(jnp.int32, (128, t), 0)
        qs_ref[qi, :, 0:t] = jnp.where(sub < A_DK, q_t, 0.0).astype(BF16)
        qs_ref[qi, :, t:2 * t] = jnp.where(sub >= A_DK, q_t, 0.0).astype(BF16)
        l_ref[qi] = jnp.zeros((1, 2 * t), F32)
        acc_ref[qi] = jnp.zeros((128, 2 * t), F32)
        if online:
            m_ref[qi] = jnp.full((1, 2 * t), NEG_BIG, F32)

        j = 0
        while qi + 1 - j > nbmax:
            step(qi, j, nbmax, False)
            j += nbmax
        step(qi, j, qi + 1 - j, True)

        accn = acc_ref[qi] / l_ref[qi]
        o_t = accn[:, 0:t] - lam * accn[:, t:2 * t]
        ms = jnp.mean(o_t * o_t, axis=0, keepdims=True)
        o = (o_t * lax.rsqrt(ms + NORM_EPS)).T
        out_ref[qi * t:(qi + 1) * t, :] = (o * gd).astype(BF16)


def _attn(c, P, lam_p, gd, *, lam_init, B, S, online):
    T = B * S
    t = ATTN_T
    nq = S // t
    in_specs = [pl.BlockSpec((S, 128), lambda b, h: (b, COL_AQ // 128 + h)),
                pl.BlockSpec((S, 128), lambda b, h: (b, COL_AK // 128 + h)),
                pl.BlockSpec((S, 128), lambda b, h: (b, COL_AV // 128 + h)),
                pl.BlockSpec((4, A_DK), lambda b, h: (0, 0)),
                pl.BlockSpec((1, A_DV), lambda b, h: (0, 0))]
    scratch = [pltpu.VMEM((nq, 128, 2 * t), BF16),
               pltpu.VMEM((128, S), BF16),
               pltpu.VMEM((nq, 1, 2 * t), F32),
               pltpu.VMEM((nq, 128, 2 * t), F32)]
    args = (P, P, P, lam_p, gd)
    if online:
        scratch = scratch + [pltpu.VMEM((nq, 1, 2 * t), F32)]
    else:
        in_specs = [pl.BlockSpec(memory_space=pltpu.SMEM)] + in_specs
        args = (c,) + args
    return pl.pallas_call(
        functools.partial(_attn_kernel, lam_init=lam_init, online=online),
        grid=(B, A_HEADS),
        in_specs=in_specs,
        out_specs=pl.BlockSpec((S, 128), lambda b, h: (b, h)),
        out_shape=jax.ShapeDtypeStruct((T, A_HEADS * A_DV), BF16),
        scratch_shapes=scratch,
        compiler_params=_params(2),
        name="attn_online" if online else "attn",
    )(*args)


def _merge_kernel(hm_ref, pu_ref, ao_ref, gm_ref, gp_ref, ga_ref, x_ref, wm_ref, wp_ref, ps_ref,
                  wd_ref, wo_ref, bm_ref, bh_ref, out_ref, halo_ref, mrg_ref, win_ref,
                  *, tiles_per_seq):
    tm = x_ref.shape[0]
    G = POOL_G
    i = pl.program_id(0)
    tile_in_seq = i % tiles_per_seq

    @pl.when(tile_in_seq == 0)
    def _():
        halo_ref[...] = jnp.zeros(halo_ref.shape, BF16)

    pos1 = tile_in_seq * tm + lax.broadcasted_iota(jnp.int32, (tm, 1), 0) + 1

    tb = bm_ref.shape[1]
    n_win = len(POOL_WINDOWS)
    for r in range(tm // tb):
        for gi in range(n_win):
            cols = slice(gi * G, (gi + 1) * G)
            prev = halo_ref[:, cols] if r == 0 else pu_ref[r * tb - POOL_HALO:r * tb, cols]
            win = win_ref.at[r * n_win + gi]
            win[...] = jnp.dot(bm_ref[gi], pu_ref[r * tb:(r + 1) * tb, cols],
                               preferred_element_type=F32)
            win[0:POOL_HALO, :] += jnp.dot(bh_ref[gi], prev, preferred_element_type=F32)

    y_m = jnp.dot(hm_ref[...], wm_ref[...], preferred_element_type=F32)
    y_a = jnp.dot(ao_ref[...], wd_ref[...], preferred_element_type=F32)
    mrg_ref[...] = gm_ref[...].astype(F32) * y_m + ga_ref[...].astype(F32) * y_a

    for r in range(tm // tb):
        rows = slice(r * tb, (r + 1) * tb)
        for gi, w in enumerate(POOL_WINDOWS):
            cols = slice(gi * G, (gi + 1) * G)
            cnt = jnp.minimum(pos1[rows], w).astype(F32)
            pooled = win_ref[r * n_win + gi] / cnt - pu_ref[rows, cols].astype(F32)
            y_p = (jnp.dot(pooled.astype(BF16), wp_ref[gi], preferred_element_type=F32)
                   * ps_ref[:, cols])
            mrg_ref[rows, cols] += gp_ref[rows, cols].astype(F32) * y_p
    halo_ref[...] = pu_ref[tm - POOL_HALO:tm, :]

    out_ref[...] = x_ref[...] + jnp.dot(mrg_ref[...].astype(BF16), wo_ref[...],
                                        preferred_element_type=F32)


def _pool_bands(tm):
    t_idx = jnp.arange(tm)[:, None]
    main, halo = [], []
    for w in POOL_WINDOWS:
        d_main = t_idx - jnp.arange(tm)[None, :]
        d_halo = t_idx[:POOL_HALO] - (jnp.arange(POOL_HALO)[None, :] - POOL_HALO)
        main.append((d_main >= 0) & (d_main < w))
        halo.append((d_halo >= 0) & (d_halo < w))
    return jnp.stack(main).astype(BF16), jnp.stack(halo).astype(BF16)


def _merge(hm, P, ao, xf, wm, wp, ps, wd, wo, B, S, l):
    T = B * S
    tm = MERGE_TM
    tb = POOL_BAND_ROWS
    band_main, band_halo = _pool_bands(tb)
    row = lambda w, cb: pl.BlockSpec((tm, w), lambda i: (i, cb))
    full = lambda shape: pl.BlockSpec(shape, lambda i: (0,) * len(shape),
                                      pipeline_mode=pl.Buffered(1))
    layer = lambda shape: pl.BlockSpec((None,) + shape, lambda i: (l,) + (0,) * len(shape),
                                       pipeline_mode=pl.Buffered(1))
    return pl.pallas_call(
        functools.partial(_merge_kernel, tiles_per_seq=S // tm),
        grid=(T // tm,),
        in_specs=[row(D_MODEL, 0), row(D_MODEL, COL_PU // D_MODEL), row(D_MODEL, 0),
                  row(D_MODEL, COL_GATE // D_MODEL), row(D_MODEL, COL_GATE // D_MODEL + 1),
                  row(D_MODEL, COL_GATE // D_MODEL + 2), row(D_MODEL, 0),
                  layer((D_MODEL, D_MODEL)), layer((4, POOL_G, POOL_G)), full((1, D_MODEL)),
                  layer((D_MODEL, D_MODEL)), layer((D_MODEL, D_MODEL)),
                  full((4, tb, tb)), full((4, POOL_HALO, POOL_HALO))],
        out_specs=row(D_MODEL, 0),
        out_shape=jax.ShapeDtypeStruct((T, D_MODEL), F32),
        scratch_shapes=[pltpu.VMEM((POOL_HALO, D_MODEL), BF16),
                        pltpu.VMEM((tm, D_MODEL), F32),
                        pltpu.VMEM((tm // tb * len(POOL_WINDOWS), tb, POOL_G), F32)],
        compiler_params=_params(1),
        name="merge",
    )(hm, P, ao, P, P, P, xf, wm, wp, ps, wd, wo, band_main, band_halo)


def _ffn_kernel(x_ref, g_ref, wgu_ref, wdn_ref, out_ref, act_ref):
    x = x_ref[...]
    ms = jnp.mean(x * x, axis=-1, keepdims=True)
    hb = (x * lax.rsqrt(ms + NORM_EPS) * g_ref[...]).astype(BF16)
    for lo, hi in FFN_CHUNKS:
        gate = jnp.dot(hb, wgu_ref[:, lo:hi], preferred_element_type=F32)
        up = jnp.dot(hb, wgu_ref[:, FF + lo:FF + hi], preferred_element_type=F32)
        act_ref[:, lo:hi] = (gate * _sigmoid(gate) * up).astype(BF16)
    out_ref[...] = x + jnp.dot(act_ref[...], wdn_ref[...], preferred_element_type=F32)


def _ffn(xf, g, wgu, wdn, l):
    T = xf.shape[0]
    tm = FFN_TM
    layer = lambda shape: pl.BlockSpec((None,) + shape, lambda i: (l, 0, 0),
                                       pipeline_mode=pl.Buffered(1))
    return pl.pallas_call(
        _ffn_kernel,
        grid=(T // tm,),
        in_specs=[pl.BlockSpec((tm, D_MODEL), lambda i: (i, 0)),
                  pl.BlockSpec((1, D_MODEL), lambda i: (0, 0)),
                  layer((D_MODEL, 2 * FF)), layer((FF, D_MODEL))],
        out_specs=pl.BlockSpec((tm, D_MODEL), lambda i: (i, 0)),
        out_shape=jax.ShapeDtypeStruct((T, D_MODEL), F32),
        scratch_shapes=[pltpu.VMEM((tm, FF), BF16)],
        compiler_params=_params(1),
        name="ffn",
    )(xf, g, wgu, wdn)


def _rope_tables(S):
    half = A_DK // 2
    inv = ROPE_THETA ** (-jnp.arange(half, dtype=F32) / half)
    ang = jnp.arange(S, dtype=F32)[:, None] * inv[None, :]
    cos, sin = jnp.cos(ang), jnp.sin(ang)
    cos_t = jnp.concatenate([cos, cos, cos, cos], axis=1)
    sin_t = jnp.concatenate([-sin, sin, -sin, sin], axis=1)
    return cos_t, sin_t


def kernel(x, g_mix, w_in, b_if, conv_qk, w_m_out, w_pool, pool_scale, g_qk, lam_p, g_diff_head,
           w_diff_out, w_out, g_ffn, w_gate_up, w_down):
    B, S, _ = x.shape
    T = B * S
    xf = x.reshape(T, D_MODEL)
    cos_t, sin_t = _rope_tables(S)
    lane = jnp.arange(256)
    gmat = (lane[:, None] // A_DK == lane[None, :] // A_DK).astype(BF16)
    n_gate = 2 * M_HEADS
    gate_lo = 4 * M_HEADS * M_DH

    w_a = w_in[:, :, :gate_lo].astype(BF16)
    w_b = w_in[:, :, gate_lo + n_gate:].astype(BF16)
    wif_t = jnp.pad(jnp.swapaxes(w_in[:, :, gate_lo:gate_lo + n_gate], 1, 2),
                    ((0, 0), (0, GATE_ROWS - n_gate), (0, 0))).astype(BF16)
    wm, wp, wd, wo = (w.astype(BF16) for w in (w_m_out, w_pool, w_diff_out, w_out))
    wgu, wdn = w_gate_up.astype(BF16), w_down.astype(BF16)

    for l in range(DEPTH):
        bif = jnp.pad(b_if[l], (0, GATE_ROWS - n_gate)).reshape(GATE_ROWS, 1)
        g2 = jnp.concatenate([g_qk[l], g_qk[l]], axis=1)
        lam_init = 0.8 - 0.6 * math.exp(-0.3 * l)

        P, gates = _proj(xf, g_mix[l].reshape(1, D_MODEL), w_a, w_b, wif_t, bif, cos_t, sin_t,
                         g2, gmat, S, l)
        hm = _mlstm(P, gates, conv_qk[l], B, S)
        c = (A_DK ** 0.5 * math.log2(math.e)) * jnp.max(jnp.abs(g_qk[l][0])) * jnp.max(jnp.abs(g_qk[l][1]))
        attn_args = (c.reshape(1), P, lam_p[l], g_diff_head[l].reshape(1, A_DV))
        ao = lax.cond(c < ATTN_MAX_STATIC_BOUND,
                      functools.partial(_attn, lam_init=lam_init, B=B, S=S, online=False),
                      functools.partial(_attn, lam_init=lam_init, B=B, S=S, online=True),
                      *attn_args)
        x1 = _merge(hm, P, ao, xf, wm, wp, pool_scale[l].reshape(1, D_MODEL), wd, wo, B, S, l)
        xf = _ffn(x1, g_ffn[l].reshape(1, D_MODEL), wgu, wdn, l)
    return xf.reshape(B, S, D_MODEL)
```

```python
import functools
import math

import jax
import jax.numpy as jnp
from jax import lax
from jax.experimental import pallas as pl
from jax.experimental.pallas import tpu as pltpu

D_MODEL = 1024
DEPTH = 2
M_HEADS = 4
M_DH = 256
M_CHUNK = 128
CONV_W = 4
POOL_WINDOWS = (2, 4, 8, 16)
POOL_G = 256
POOL_HALO = 16
A_HEADS = 8
A_DK = 64
A_DV = 128
FF = 2816
NORM_EPS = 1e-6
NEG_BIG = -1e30
ROPE_THETA = 10000.0

COL_MQ, COL_MK, COL_MV, COL_MO = 0, 1024, 2048, 3072
COL_PU, COL_AQ, COL_AK, COL_AV, COL_GATE = 4096, 5120, 6144, 7168, 8192
N_MAIN = 11264
GATE_ROWS = 16

BF16 = jnp.bfloat16
F32 = jnp.float32

VMEM_LIMIT = 56 * 1024 * 1024

PROJ_TM, PROJ_TN = 256, 1024
MLSTM_CHUNKS_PER_STEP = 4
ATTN_T = 512
ATTN_BLOCKS_PER_TRIP = 4
ATTN_MAX_STATIC_BOUND = 60.0
MERGE_TM = 512
POOL_BAND_ROWS = 256
FFN_TM = 512
FFN_CHUNKS = ((0, 512), (512, 1024), (1024, 1536), (1536, 2048), (2048, 2560), (2560, 2816))


def _params(n_axes):
    return pltpu.CompilerParams(dimension_semantics=("arbitrary",) * n_axes,
                                vmem_limit_bytes=VMEM_LIMIT)


def _sigmoid(x):
    return 1.0 / (1.0 + jnp.exp(-x))


def _proj_kernel(x_ref, g_ref, wa_ref, wb_ref, wif_ref, bif_ref, cos_ref, sin_ref, gqk_ref, gm_ref,
                 out_ref, gates_ref):
    tm = x_ref.shape[0]
    tn = PROJ_TN
    L, H = M_CHUNK, M_HEADS

    x = x_ref[...]
    ms = jnp.mean(x * x, axis=-1, keepdims=True)
    hb = (x * lax.rsqrt(ms + NORM_EPS) * g_ref[...]).astype(BF16)

    gt = lax.dot_general(wif_ref[...], hb, (((1,), (1,)), ((), ())),
                         preferred_element_type=F32) + bif_ref[...]
    lf = jnp.minimum(gt, 0.0) - jnp.log(1.0 + jnp.exp(-jnp.abs(gt)))
    lane = lax.broadcasted_iota(jnp.int32, (GATE_ROWS, L), 1)
    is_input_gate = lax.broadcasted_iota(jnp.int32, (GATE_ROWS, L), 0) < H
    for c in range(tm // L):
        b = lf[:, c * L:(c + 1) * L]
        for sh in (1, 2, 4, 8, 16, 32, 64):
            b = b + jnp.where(lane >= sh, pltpu.roll(b, sh, axis=1), 0.0)
        gates_ref[:, c * L:(c + 1) * L] = jnp.where(is_input_gate, gt[:, c * L:(c + 1) * L], b)

    lane_r = lax.broadcasted_iota(jnp.int32, (tm, 128), 1)
    first_half = (lane_r & (A_DK // 2)) == 0
    cos = cos_ref[...]
    sin = sin_ref[...]

    n_a = wa_ref.shape[1] // tn
    for j in range(N_MAIN // tn):
        w_tile = wa_ref[:, j * tn:(j + 1) * tn] if j < n_a else wb_ref[:, (j - n_a) * tn:(j - n_a + 1) * tn]
        acc = jnp.dot(hb, w_tile, preferred_element_type=F32)
        c0 = j * tn
        if c0 in (COL_MO, COL_GATE, COL_GATE + tn, COL_GATE + 2 * tn):
            out_ref[:, c0:c0 + tn] = _sigmoid(acc).astype(BF16)
        elif c0 in (COL_AQ, COL_AK):
            g = (gqk_ref[0:1, :] * (A_DK ** -0.5 * math.log2(math.e)) if c0 == COL_AQ
                 else gqk_ref[1:2, :])
            for c2 in range(A_HEADS // 2):
                x2 = acc[:, c2 * 256:(c2 + 1) * 256]
                ss2 = jnp.dot((x2 * x2).astype(BF16), gm_ref[...],
                              preferred_element_type=F32)
                y2 = x2 * lax.rsqrt(ss2 * (1.0 / A_DK) + NORM_EPS)
                for c in (2 * c2, 2 * c2 + 1):
                    y = y2[:, (c % 2) * 128:(c % 2 + 1) * 128] * g
                    partner = jnp.where(first_half, pltpu.roll(y, 128 - A_DK // 2, axis=1),
                                        pltpu.roll(y, A_DK // 2, axis=1))
                    out_ref[:, c0 + c * 128:c0 + (c + 1) * 128] = (
                        y * cos + partner * sin).astype(BF16)
        else:
            out_ref[:, c0:c0 + tn] = acc.astype(BF16)


def _proj(xf, g, w_all, w_b, wif_t, bif, cos_t, sin_t, g2, gmat, S, l):
    T = xf.shape[0]
    tm = PROJ_TM
    tps = S // tm
    const = lambda shape: pl.BlockSpec(shape, lambda i: (0, 0))
    layer = lambda shape: pl.BlockSpec((None,) + shape, lambda i: (l, 0, 0),
                                       pipeline_mode=pl.Buffered(1))
    return pl.pallas_call(
        _proj_kernel,
        grid=(T // tm,),
        in_specs=[pl.BlockSpec((tm, D_MODEL), lambda i: (i, 0)),
                  const((1, D_MODEL)),
                  layer((D_MODEL, N_MAIN - w_b.shape[2])), layer((D_MODEL, w_b.shape[2])),
                  layer((GATE_ROWS, D_MODEL)),
                  const((GATE_ROWS, 1)),
                  pl.BlockSpec((tm, 128), lambda i: (i % tps, 0)),
                  pl.BlockSpec((tm, 128), lambda i: (i % tps, 0)),
                  const((2, 128)), const((256, 256))],
        out_specs=[pl.BlockSpec((tm, N_MAIN), lambda i: (i, 0)),
                   pl.BlockSpec((GATE_ROWS, tm), lambda i: (0, i))],
        out_shape=[jax.ShapeDtypeStruct((T, N_MAIN), BF16),
                   jax.ShapeDtypeStruct((GATE_ROWS, T), F32)],
        compiler_params=_params(1),
        name="proj",
    )(xf, g, w_all, w_b, wif_t, bif, cos_t, sin_t, g2, gmat)


def _mlstm_kernel(q_ref, k_ref, v_ref, o_ref, gt_ref, cw_ref, shift_ref, out_ref, ubuf, c_ref, n_ref,
                  m_ref):
    L, H, Dh = M_CHUNK, M_HEADS, M_DH
    W = H * Dh
    R = q_ref.shape[0]
    NC = R // L
    pairs = [(cc, h) for cc in range(NC) for h in range(H)]

    @pl.when(pl.program_id(1) == 0)
    def _():
        ubuf[0:L, :] = jnp.zeros((L, 2 * W), BF16)
        c_ref[...] = jnp.zeros(c_ref.shape, F32)
        n_ref[...] = jnp.zeros(n_ref.shape, F32)
        m_ref[...] = jnp.full(m_ref.shape, NEG_BIG, F32)

    ubuf[L:L + R, 0:W] = q_ref[...]
    ubuf[L:L + R, W:2 * W] = k_ref[...]

    row = lax.broadcasted_iota(jnp.int32, (L, L), 0)
    col = lax.broadcasted_iota(jnp.int32, (L, L), 1)
    eye = row == col
    tril = row >= col

    def to_col(r):
        return jnp.sum(jnp.where(eye, jnp.broadcast_to(r, (L, L)), 0.0), axis=1, keepdims=True)

    def conv_silu(r0, c0):
        cols = slice(c0, c0 + Dh)
        u = ubuf[r0:r0 + 2 * L, cols]
        taps = jnp.concatenate([u * cw_ref[CONV_W - 1 - s:CONV_W - s, cols] for s in range(CONV_W)],
                               axis=0)
        y = jnp.dot(shift_ref[...], taps, preferred_element_type=F32)
        return y / (1.0 + jnp.exp2(y * (-math.log2(math.e))))

    rows_of = lambda cc: slice(cc * L, (cc + 1) * L)
    cols_of = lambda h: slice(h * Dh, (h + 1) * Dh)

    g_tot, rowterm, m_loc, e_col, b_col = {}, {}, {}, {}, {}
    e_row, b_row = {}, {}
    for cc, h in pairs:
        li = gt_ref[h:h + 1, rows_of(cc)]
        b_row[cc, h] = gt_ref[H + h:H + h + 1, rows_of(cc)]
        g_tot[cc, h] = b_row[cc, h][:, L - 1:L]
        rowterm[cc, h] = li - b_row[cc, h]
        w_state = g_tot[cc, h] + rowterm[cc, h]
        m_loc[cc, h] = jnp.max(w_state, axis=1, keepdims=True)
        e_row[cc, h] = jnp.exp(w_state - m_loc[cc, h])
    for cc, h in pairs:
        e_col[cc, h] = to_col(e_row[cc, h])
        b_col[cc, h] = to_col(b_row[cc, h])

    m_prev, a_dec, b_in = {}, {}, {}
    for h in range(H):
        m = m_ref[h:h + 1, 0:1]
        for cc in range(NC):
            m_prev[cc, h] = m
            m_new = jnp.maximum(g_tot[cc, h] + m, m_loc[cc, h])
            a_dec[cc, h] = jnp.exp(g_tot[cc, h] + m - m_new)
            b_in[cc, h] = jnp.exp(m_loc[cc, h] - m_new)
            m = m_new
        m_ref[h:h + 1, :] = jnp.broadcast_to(m, (1, 128))

    qh, kh, qb, kb = {}, {}, {}, {}
    for cc, h in pairs:
        qh[cc, h] = conv_silu(cc * L, h * Dh) * (Dh ** -0.5)
        kh[cc, h] = conv_silu(cc * L, W + h * Dh)
        qb[cc, h] = qh[cc, h].astype(BF16)
        kb[cc, h] = kh[cc, h].astype(BF16)

    raw = {}
    for cc, h in pairs:
        raw[cc, h] = lax.dot_general(qb[cc, h], kb[cc, h], (((1,), (1,)), ((), ())),
                                     preferred_element_type=F32)
    inter, sv, den0, floor = {}, {}, {}, {}
    for cc, h in pairs:
        dm = jnp.where(tril, b_col[cc, h] + rowterm[cc, h], NEG_BIG)
        m_intra = jnp.max(dm, axis=1, keepdims=True)
        m_inter = b_col[cc, h] + m_prev[cc, h]
        m_t = jnp.maximum(m_inter, m_intra)
        sqk = raw[cc, h] * jnp.exp(dm - m_t)
        inter[cc, h] = jnp.exp(m_inter - m_t)
        floor[cc, h] = jnp.exp(-m_t)
        den0[cc, h] = jnp.sum(sqk, axis=1, keepdims=True)
        sv[cc, h] = jnp.dot(sqk.astype(BF16), v_ref[rows_of(cc), cols_of(h)],
                            preferred_element_type=F32)

    for cc in range(NC):
        c_loc, n_loc = {}, {}
        for h in range(H):
            ke = kh[cc, h] * e_col[cc, h]
            c_loc[h] = lax.dot_general(ke.astype(BF16), v_ref[rows_of(cc), cols_of(h)],
                                       (((0,), (0,)), ((), ())), preferred_element_type=F32)
            n_loc[h] = jnp.sum(ke, axis=0, keepdims=True)
        for h in range(H):
            c_prev = c_ref[h]
            n_prev = n_ref[h:h + 1, :]
            q_c = jnp.dot(qb[cc, h], c_prev.astype(BF16), preferred_element_type=F32)
            q_n = jnp.sum(qh[cc, h] * n_prev, axis=1, keepdims=True)
            num = sv[cc, h] + inter[cc, h] * q_c
            den = den0[cc, h] + inter[cc, h] * q_n
            hh = num / jnp.maximum(jnp.abs(den), floor[cc, h])
            out_ref[rows_of(cc), cols_of(h)] = (
                o_ref[rows_of(cc), cols_of(h)].astype(F32) * hh).astype(BF16)
            c_ref[h] = a_dec[cc, h] * c_prev + b_in[cc, h] * c_loc[h]
            n_ref[h:h + 1, :] = a_dec[cc, h] * n_prev + b_in[cc, h] * n_loc[h]

    ubuf[0:L, :] = ubuf[R:R + L, :]


def _mlstm(P, gates, conv_w, B, S):
    T = B * S
    L, H, Dh = M_CHUNK, M_HEADS, M_DH
    R = MLSTM_CHUNKS_PER_STEP * L
    ns = S // R
    W = H * Dh
    blk = lambda cb: pl.BlockSpec((R, W), lambda b, c: (b * ns + c, cb))
    j_idx = jnp.arange(CONV_W * 2 * L) % (2 * L)
    s_idx = jnp.arange(CONV_W * 2 * L) // (2 * L)
    shift = (j_idx[None, :] == (L + jnp.arange(L)[:, None] - s_idx[None, :])).astype(BF16)
    return pl.pallas_call(
        _mlstm_kernel,
        grid=(B, ns),
        in_specs=[blk(COL_MQ // W), blk(COL_MK // W), blk(COL_MV // W), blk(COL_MO // W),
                  pl.BlockSpec((GATE_ROWS, R), lambda b, c: (0, b * ns + c)),
                  pl.BlockSpec((CONV_W, 2 * W), lambda b, c: (0, 0)),
                  pl.BlockSpec((L, CONV_W * 2 * L), lambda b, c: (0, 0))],
        out_specs=pl.BlockSpec((R, W), lambda b, c: (b * ns + c, 0)),
        out_shape=jax.ShapeDtypeStruct((T, W), BF16),
        scratch_shapes=[pltpu.VMEM((L + R, 2 * W), BF16),
                        pltpu.VMEM((H, Dh, Dh), F32),
                        pltpu.VMEM((8, Dh), F32),
                        pltpu.VMEM((8, 128), F32)],
        compiler_params=_params(2),
        name="mlstm",
    )(P, P, P, P, gates, conv_w.astype(BF16), shift)


def _attn_kernel(*refs, lam_init, online):
    if online:
        q_ref, k_ref, v_ref, lam_ref, gd_ref, out_ref, qs_ref, vt_ref, l_ref, acc_ref, m_ref = refs
    else:
        c_ref, q_ref, k_ref, v_ref, lam_ref, gd_ref, out_ref, qs_ref, vt_ref, l_ref, acc_ref = refs
    t = ATTN_T
    nq = q_ref.shape[0] // t
    nbmax = ATTN_BLOCKS_PER_TRIP

    lp = lam_ref[...]
    lam = (jnp.exp(jnp.sum(lp[0:1, :] * lp[1:2, :], axis=1, keepdims=True))
           - jnp.exp(jnp.sum(lp[2:3, :] * lp[3:4, :], axis=1, keepdims=True)) + lam_init)
    gd = gd_ref[...] * (1.0 - lam_init)

    for c in range(nq):
        vt_ref[:, c * t:(c + 1) * t] = v_ref[c * t:(c + 1) * t, :].astype(F32).T.astype(BF16)

    def step(qi, j, nb, diag_last):
        k = k_ref[j * t:(j + nb) * t, :]
        vt = vt_ref[:, j * t:(j + nb) * t]
        s = jnp.dot(k, qs_ref[qi], preferred_element_type=F32)
        if diag_last:
            row = lax.broadcasted_iota(jnp.int32, (nb * t, 2 * t), 0) - (nb - 1) * t
            col = lax.broadcasted_iota(jnp.int32, (nb * t, 2 * t), 1)
            qcol = jnp.where(col >= t, col - t, col)
            s = jnp.where(row <= qcol, s, NEG_BIG)
        if online:
            m_prev = m_ref[qi]
            m_new = jnp.maximum(m_prev, jnp.max(s, axis=0, keepdims=True))
            alpha = jnp.exp2(m_prev - m_new)
            p = jnp.exp2(s - m_new)
            l_ref[qi] = alpha * l_ref[qi] + jnp.sum(p, axis=0, keepdims=True)
            acc_ref[qi] = alpha * acc_ref[qi] + jnp.dot(vt, p.astype(BF16),
                                                        preferred_element_type=F32)
            m_ref[qi] = m_new
        else:
            p = jnp.exp2(s - c_ref[0])
            l_ref[qi] += jnp.sum(p, axis=0, keepdims=True)
            acc_ref[qi] += jnp.dot(vt, p.astype(BF16), preferred_element_type=F32)

    for qi in range(nq):
        q_t = q_ref[qi * t:(qi + 1) * t, :].astype(F32).T
        sub = lax.broadcasted_iota(jnp.int32, (128, t), 0)
        qs_ref[qi, :, 0:t] = jnp.where(sub < A_DK, q_t, 0.0).astype(BF16)
        qs_ref[qi, :, t:2 * t] = jnp.where(sub >= A_DK, q_t, 0.0).astype(BF16)
        l_ref[qi] = jnp.zeros((1, 2 * t), F32)
        acc_ref[qi] = jnp.zeros((128, 2 * t), F32)
        if online:
            m_ref[qi] = jnp.full((1, 2 * t), NEG_BIG, F32)

        j = 0
        while qi + 1 - j > nbmax:
            step(qi, j, nbmax, False)
            j += nbmax
        step(qi, j, qi + 1 - j, True)

        accn = acc_ref[qi] / l_ref[qi]
        o_t = accn[:, 0:t] - lam * accn[:, t:2 * t]
        ms = jnp.mean(o_t * o_t, axis=0, keepdims=True)
        o = (o_t * lax.rsqrt(ms + NORM_EPS)).T
        out_ref[qi * t:(qi + 1) * t, :] = (o * gd).astype(BF16)


def _attn(c, P, lam_p, gd, *, lam_init, B, S, online):
    T = B * S
    t = ATTN_T
    nq = S // t
    in_specs = [pl.BlockSpec((S, 128), lambda b, h: (b, COL_AQ // 128 + h)),
                pl.BlockSpec((S, 128), lambda b, h: (b, COL_AK // 128 + h)),
                pl.BlockSpec((S, 128), lambda b, h: (b, COL_AV // 128 + h)),
                pl.BlockSpec((4, A_DK), lambda b, h: (0, 0)),
                pl.BlockSpec((1, A_DV), lambda b, h: (0, 0))]
    scratch = [pltpu.VMEM((nq, 128, 2 * t), BF16),
               pltpu.VMEM((128, S), BF16),
               pltpu.VMEM((nq, 1, 2 * t), F32),
               pltpu.VMEM((nq, 128, 2 * t), F32)]
    args = (P, P, P, lam_p, gd)
    if online:
        scratch = scratch + [pltpu.VMEM((nq, 1, 2 * t), F32)]
    else:
        in_specs = [pl.BlockSpec(memory_space=pltpu.SMEM)] + in_specs
        args = (c,) + args
    return pl.pallas_call(
        functools.partial(_attn_kernel, lam_init=lam_init, online=online),
        grid=(B, A_HEADS),
        in_specs=in_specs,
        out_specs=pl.BlockSpec((S, 128), lambda b, h: (b, h)),
        out_shape=jax.ShapeDtypeStruct((T, A_HEADS * A_DV), BF16),
        scratch_shapes=scratch,
        compiler_params=_params(2),
        name="attn_online" if online else "attn",
    )(*args)


def _merge_kernel(hm_ref, pu_ref, ao_ref, gm_ref, gp_ref, ga_ref, x_ref, wm_ref, wp_ref, ps_ref,
                  wd_ref, wo_ref, bm_ref, bh_ref, out_ref, halo_ref, mrg_ref, win_ref,
                  *, tiles_per_seq):
    tm = x_ref.shape[0]
    G = POOL_G
    i = pl.program_id(0)
    tile_in_seq = i % tiles_per_seq

    @pl.when(tile_in_seq == 0)
    def _():
        halo_ref[...] = jnp.zeros(halo_ref.shape, BF16)

    pos1 = tile_in_seq * tm + lax.broadcasted_iota(jnp.int32, (tm, 1), 0) + 1

    tb = bm_ref.shape[1]
    n_win = len(POOL_WINDOWS)
    for r in range(tm // tb):
        for gi in range(n_win):
            cols = slice(gi * G, (gi + 1) * G)
            prev = halo_ref[:, cols] if r == 0 else pu_ref[r * tb - POOL_HALO:r * tb, cols]
            win = win_ref.at[r * n_win + gi]
            win[...] = jnp.dot(bm_ref[gi], pu_ref[r * tb:(r + 1) * tb, cols],
                               preferred_element_type=F32)
            win[0:POOL_HALO, :] += jnp.dot(bh_ref[gi], prev, preferred_element_type=F32)

    y_m = jnp.dot(hm_ref[...], wm_ref[...], preferred_element_type=F32)
    y_a = jnp.dot(ao_ref[...], wd_ref[...], preferred_element_type=F32)
    mrg_ref[...] = gm_ref[...].astype(F32) * y_m + ga_ref[...].astype(F32) * y_a

    for r in range(tm // tb):
        rows = slice(r * tb, (r + 1) * tb)
        for gi, w in enumerate(POOL_WINDOWS):
            cols = slice(gi * G, (gi + 1) * G)
            cnt = jnp.minimum(pos1[rows], w).astype(F32)
            pooled = win_ref[r * n_win + gi] / cnt - pu_ref[rows, cols].astype(F32)
            y_p = (jnp.dot(pooled.astype(BF16), wp_ref[gi], preferred_element_type=F32)
                   * ps_ref[:, cols])
            mrg_ref[rows, cols] += gp_ref[rows, cols].astype(F32) * y_p
    halo_ref[...] = pu_ref[tm - POOL_HALO:tm, :]

    out_ref[...] = x_ref[...] + jnp.dot(mrg_ref[...].astype(BF16), wo_ref[...],
                                        preferred_element_type=F32)


def _pool_bands(tm):
    t_idx = jnp.arange(tm)[:, None]
    main, halo = [], []
    for w in POOL_WINDOWS:
        d_main = t_idx - jnp.arange(tm)[None, :]
        d_halo = t_idx[:POOL_HALO] - (jnp.arange(POOL_HALO)[None, :] - POOL_HALO)
        main.append((d_main >= 0) & (d_main < w))
        halo.append((d_halo >= 0) & (d_halo < w))
    return jnp.stack(main).astype(BF16), jnp.stack(halo).astype(BF16)


def _merge(hm, P, ao, xf, wm, wp, ps, wd, wo, B, S, l):
    T = B * S
    tm = MERGE_TM
    tb = POOL_BAND_ROWS
    band_main, band_halo = _pool_bands(tb)
    row = lambda w, cb: pl.BlockSpec((tm, w), lambda i: (i, cb))
    full = lambda shape: pl.BlockSpec(shape, lambda i: (0,) * len(shape),
                                      pipeline_mode=pl.Buffered(1))
    layer = lambda shape: pl.BlockSpec((None,) + shape, lambda i: (l,) + (0,) * len(shape),
                                       pipeline_mode=pl.Buffered(1))
    return pl.pallas_call(
        functools.partial(_merge_kernel, tiles_per_seq=S // tm),
        grid=(T // tm,),
        in_specs=[row(D_MODEL, 0), row(D_MODEL, COL_PU // D_MODEL), row(D_MODEL, 0),
                  row(D_MODEL, COL_GATE // D_MODEL), row(D_MODEL, COL_GATE // D_MODEL + 1),
                  row(D_MODEL, COL_GATE // D_MODEL + 2), row(D_MODEL, 0),
                  layer((D_MODEL, D_MODEL)), layer((4, POOL_G, POOL_G)), full((1, D_MODEL)),
                  layer((D_MODEL, D_MODEL)), layer((D_MODEL, D_MODEL)),
                  full((4, tb, tb)), full((4, POOL_HALO, POOL_HALO))],
        out_specs=row(D_MODEL, 0),
        out_shape=jax.ShapeDtypeStruct((T, D_MODEL), F32),
        scratch_shapes=[pltpu.VMEM((POOL_HALO, D_MODEL), BF16),
                        pltpu.VMEM((tm, D_MODEL), F32),
                        pltpu.VMEM((tm // tb * len(POOL_WINDOWS), tb, POOL_G), F32)],
        compiler_params=_params(1),
        name="merge",
    )(hm, P, ao, P, P, P, xf, wm, wp, ps, wd, wo, band_main, band_halo)


def _ffn_kernel(x_ref, g_ref, wgu_ref, wdn_ref, out_ref, act_ref):
    x = x_ref[...]
    ms = jnp.mean(x * x, axis=-1, keepdims=True)
    hb = (x * lax.rsqrt(ms + NORM_EPS) * g_ref[...]).astype(BF16)
    for lo, hi in FFN_CHUNKS:
        gate = jnp.dot(hb, wgu_ref[:, lo:hi], preferred_element_type=F32)
        up = jnp.dot(hb, wgu_ref[:, FF + lo:FF + hi], preferred_element_type=F32)
        act_ref[:, lo:hi] = (gate * _sigmoid(gate) * up).astype(BF16)
    out_ref[...] = x + jnp.dot(act_ref[...], wdn_ref[...], preferred_element_type=F32)


def _ffn(xf, g, wgu, wdn, l):
    T = xf.shape[0]
    tm = FFN_TM
    layer = lambda shape: pl.BlockSpec((None,) + shape, lambda i: (l, 0, 0),
                                       pipeline_mode=pl.Buffered(1))
    return pl.pallas_call(
        _ffn_kernel,
        grid=(T // tm,),
        in_specs=[pl.BlockSpec((tm, D_MODEL), lambda i: (i, 0)),
                  pl.BlockSpec((1, D_MODEL), lambda i: (0, 0)),
                  layer((D_MODEL, 2 * FF)), layer((FF, D_MODEL))],
        out_specs=pl.BlockSpec((tm, D_MODEL), lambda i: (i, 0)),
        out_shape=jax.ShapeDtypeStruct((T, D_MODEL), F32),
        scratch_shapes=[pltpu.VMEM((tm, FF), BF16)],
        compiler_params=_params(1),
        name="ffn",
    )(xf, g, wgu, wdn)


def _rope_tables(S):
    half = A_DK // 2
    inv = ROPE_THETA ** (-jnp.arange(half, dtype=F32) / half)
    ang = jnp.arange(S, dtype=F32)[:, None] * inv[None, :]
    cos, sin = jnp.cos(ang), jnp.sin(ang)
    cos_t = jnp.concatenate([cos, cos, cos, cos], axis=1)
    sin_t = jnp.concatenate([-sin, sin, -sin, sin], axis=1)
    return cos_t, sin_t


def kernel(x, g_mix, w_in, b_if, conv_qk, w_m_out, w_pool, pool_scale, g_qk, lam_p, g_diff_head,
           w_diff_out, w_out, g_ffn, w_gate_up, w_down):
    B, S, _ = x.shape
    T = B * S
    xf = x.reshape(T, D_MODEL)
    cos_t, sin_t = _rope_tables(S)
    lane = jnp.arange(256)
    gmat = (lane[:, None] // A_DK == lane[None, :] // A_DK).astype(BF16)
    n_gate = 2 * M_HEADS
    gate_lo = 4 * M_HEADS * M_DH

    w_all = w_in.astype(BF16)
    w_b = w_all[:, :, gate_lo + n_gate:]
    wif_t = jnp.pad(jnp.swapaxes(w_all[:, :, gate_lo:gate_lo + n_gate], 1, 2),
                    ((0, 0), (0, GATE_ROWS - n_gate), (0, 0)))
    wm, wp, wd, wo = (w.astype(BF16) for w in (w_m_out, w_pool, w_diff_out, w_out))
    wgu, wdn = w_gate_up.astype(BF16), w_down.astype(BF16)

    for l in range(DEPTH):
        bif = jnp.pad(b_if[l], (0, GATE_ROWS - n_gate)).reshape(GATE_ROWS, 1)
        g2 = jnp.concatenate([g_qk[l], g_qk[l]], axis=1)
        lam_init = 0.8 - 0.6 * math.exp(-0.3 * l)

        P, gates = _proj(xf, g_mix[l].reshape(1, D_MODEL), w_all, w_b, wif_t, bif, cos_t, sin_t,
                         g2, gmat, S, l)
        hm = _mlstm(P, gates, conv_qk[l], B, S)
        c = (A_DK ** 0.5 * math.log2(math.e)) * jnp.max(jnp.abs(g_qk[l][0])) * jnp.max(jnp.abs(g_qk[l][1]))
        attn_args = (c.reshape(1), P, lam_p[l], g_diff_head[l].reshape(1, A_DV))
        ao = lax.cond(c < ATTN_MAX_STATIC_BOUND,
                      functools.partial(_attn, lam_init=lam_init, B=B, S=S, online=False),
                      functools.partial(_attn, lam_init=lam_init, B=B, S=S, online=True),
                      *attn_args)
        x1 = _merge(hm, P, ao, xf, wm, wp, pool_scale[l].reshape(1, D_MODEL), wd, wo, B, S, l)
        xf = _ffn(x1, g_ffn[l].reshape(1, D_MODEL), wgu, wdn, l)
    return xf.reshape(B, S, D_MODEL)
```

```python
import functools
import math

import jax
import jax.numpy as jnp
from jax import lax
from jax.experimental import pallas as pl
from jax.experimental.pallas import tpu as pltpu

D_MODEL = 1024
DEPTH = 2
M_HEADS = 4
M_DH = 256
M_CHUNK = 128
CONV_W = 4
POOL_WINDOWS = (2, 4, 8, 16)
POOL_G = 256
POOL_HALO = 16
A_HEADS = 8
A_DK = 64
A_DV = 128
FF = 2816
NORM_EPS = 1e-6
NEG_BIG = -1e30
ROPE_THETA = 10000.0

COL_MQ, COL_MK, COL_MV, COL_MO = 0, 1024, 2048, 3072
COL_PU, COL_AQ, COL_AK, COL_AV, COL_GATE = 4096, 5120, 6144, 7168, 8192
N_MAIN = 11264
GATE_ROWS = 16

BF16 = jnp.bfloat16
F32 = jnp.float32

VMEM_LIMIT = 56 * 1024 * 1024

PROJ_TM, PROJ_TN = 256, 1024
MLSTM_CHUNKS_PER_STEP = 4
ATTN_T = 512
ATTN_BLOCKS_PER_TRIP = 4
ATTN_MAX_STATIC_BOUND = 60.0
MERGE_TM = 512
POOL_BAND_ROWS = 256
FFN_TM = 512
FFN_CHUNKS = ((0, 512), (512, 1024), (1024, 1536), (1536, 2048), (2048, 2560), (2560, 2816))


def _params(n_axes):
    return pltpu.CompilerParams(dimension_semantics=("arbitrary",) * n_axes,
                                vmem_limit_bytes=VMEM_LIMIT)


def _sigmoid(x):
    return 1.0 / (1.0 + jnp.exp(-x))


def _proj_kernel(x_ref, g_ref, wa_ref, wb_ref, wif_ref, bif_ref, cos_ref, sin_ref, gqk_ref, gm_ref,
                 out_ref, gates_ref):
    tm = x_ref.shape[0]
    tn = PROJ_TN
    L, H = M_CHUNK, M_HEADS

    x = x_ref[...]
    ms = jnp.mean(x * x, axis=-1, keepdims=True)
    hb = (x * lax.rsqrt(ms + NORM_EPS) * g_ref[...]).astype(BF16)

    gt = lax.dot_general(wif_ref[...], hb, (((1,), (1,)), ((), ())),
                         preferred_element_type=F32) + bif_ref[...]
    lf = jnp.minimum(gt, 0.0) - jnp.log(1.0 + jnp.exp(-jnp.abs(gt)))
    lane = lax.broadcasted_iota(jnp.int32, (GATE_ROWS, L), 1)
    is_input_gate = lax.broadcasted_iota(jnp.int32, (GATE_ROWS, L), 0) < H
    for c in range(tm // L):
        b = lf[:, c * L:(c + 1) * L]
        for sh in (1, 2, 4, 8, 16, 32, 64):
            b = b + jnp.where(lane >= sh, pltpu.roll(b, sh, axis=1), 0.0)
        gates_ref[:, c * L:(c + 1) * L] = jnp.where(is_input_gate, gt[:, c * L:(c + 1) * L], b)

    lane_r = lax.broadcasted_iota(jnp.int32, (tm, 128), 1)
    first_half = (lane_r & (A_DK // 2)) == 0
    cos = cos_ref[...]
    sin = sin_ref[...]

    n_a = wa_ref.shape[1] // tn
    for j in range(N_MAIN // tn):
        w_tile = wa_ref[:, j * tn:(j + 1) * tn] if j < n_a else wb_ref[:, (j - n_a) * tn:(j - n_a + 1) * tn]
        acc = jnp.dot(hb, w_tile, preferred_element_type=F32)
        c0 = j * tn
        if c0 in (COL_MO, COL_GATE, COL_GATE + tn, COL_GATE + 2 * tn):
            out_ref[:, c0:c0 + tn] = _sigmoid(acc).astype(BF16)
        elif c0 in (COL_AQ, COL_AK):
            g = (gqk_ref[0:1, :] * (A_DK ** -0.5 * math.log2(math.e)) if c0 == COL_AQ
                 else gqk_ref[1:2, :])
            for c2 in range(A_HEADS // 2):
                x2 = acc[:, c2 * 256:(c2 + 1) * 256]
                ss2 = jnp.dot((x2 * x2).astype(BF16), gm_ref[...],
                              preferred_element_type=F32)
                y2 = x2 * lax.rsqrt(ss2 * (1.0 / A_DK) + NORM_EPS)
                for c in (2 * c2, 2 * c2 + 1):
                    y = y2[:, (c % 2) * 128:(c % 2 + 1) * 128] * g
                    partner = jnp.where(first_half, pltpu.roll(y, 128 - A_DK // 2, axis=1),
                                        pltpu.roll(y, A_DK // 2, axis=1))
                    out_ref[:, c0 + c * 128:c0 + (c + 1) * 128] = (
                        y * cos + partner * sin).astype(BF16)
        else:
            out_ref[:, c0:c0 + tn] = acc.astype(BF16)


def _proj(xf, g, w_a, w_b, wif_t, bif, cos_t, sin_t, g2, gmat, S, l):
    T = xf.shape[0]
    tm = PROJ_TM
    tps = S // tm
    const = lambda shape: pl.BlockSpec(shape, lambda i: (0, 0))
    layer = lambda shape: pl.BlockSpec((None,) + shape, lambda i: (l, 0, 0),
                                       pipeline_mode=pl.Buffered(1))
    return pl.pallas_call(
        _proj_kernel,
        grid=(T // tm,),
        in_specs=[pl.BlockSpec((tm, D_MODEL), lambda i: (i, 0)),
                  const((1, D_MODEL)),
                  layer((D_MODEL, w_a.shape[2])), layer((D_MODEL, w_b.shape[2])),
                  layer((GATE_ROWS, D_MODEL)),
                  const((GATE_ROWS, 1)),
                  pl.BlockSpec((tm, 128), lambda i: (i % tps, 0)),
                  pl.BlockSpec((tm, 128), lambda i: (i % tps, 0)),
                  const((2, 128)), const((256, 256))],
        out_specs=[pl.BlockSpec((tm, N_MAIN), lambda i: (i, 0)),
                   pl.BlockSpec((GATE_ROWS, tm), lambda i: (0, i))],
        out_shape=[jax.ShapeDtypeStruct((T, N_MAIN), BF16),
                   jax.ShapeDtypeStruct((GATE_ROWS, T), F32)],
        compiler_params=_params(1),
        name="proj",
    )(xf, g, w_a, w_b, wif_t, bif, cos_t, sin_t, g2, gmat)


def _mlstm_kernel(q_ref, k_ref, v_ref, o_ref, gt_ref, cw_ref, shift_ref, out_ref, ubuf, c_ref, n_ref,
                  m_ref):
    L, H, Dh = M_CHUNK, M_HEADS, M_DH
    W = H * Dh
    R = q_ref.shape[0]
    NC = R // L
    pairs = [(cc, h) for cc in range(NC) for h in range(H)]

    @pl.when(pl.program_id(1) == 0)
    def _():
        ubuf[0:L, :] = jnp.zeros((L, 2 * W), BF16)
        c_ref[...] = jnp.zeros(c_ref.shape, F32)
        n_ref[...] = jnp.zeros(n_ref.shape, F32)
        m_ref[...] = jnp.full(m_ref.shape, NEG_BIG, F32)

    ubuf[L:L + R, 0:W] = q_ref[...]
    ubuf[L:L + R, W:2 * W] = k_ref[...]

    row = lax.broadcasted_iota(jnp.int32, (L, L), 0)
    col = lax.broadcasted_iota(jnp.int32, (L, L), 1)
    eye = row == col
    tril = row >= col

    def to_col(r):
        return jnp.sum(jnp.where(eye, jnp.broadcast_to(r, (L, L)), 0.0), axis=1, keepdims=True)

    def conv_silu(r0, c0):
        cols = slice(c0, c0 + Dh)
        sh = jnp.dot(shift_ref[...], ubuf[r0:r0 + 2 * L, cols], preferred_element_type=F32)
        y = (cw_ref[3:4, cols] * sh[0:L] + cw_ref[2:3, cols] * sh[L:2 * L]
             + cw_ref[1:2, cols] * sh[2 * L:3 * L] + cw_ref[0:1, cols] * sh[3 * L:4 * L])
        return y / (1.0 + jnp.exp2(y * (-math.log2(math.e))))

    rows_of = lambda cc: slice(cc * L, (cc + 1) * L)
    cols_of = lambda h: slice(h * Dh, (h + 1) * Dh)

    g_tot, rowterm, m_loc, e_col, b_col = {}, {}, {}, {}, {}
    e_row, b_row = {}, {}
    for cc, h in pairs:
        li = gt_ref[h:h + 1, rows_of(cc)]
        b_row[cc, h] = gt_ref[H + h:H + h + 1, rows_of(cc)]
        g_tot[cc, h] = b_row[cc, h][:, L - 1:L]
        rowterm[cc, h] = li - b_row[cc, h]
        w_state = g_tot[cc, h] + rowterm[cc, h]
        m_loc[cc, h] = jnp.max(w_state, axis=1, keepdims=True)
        e_row[cc, h] = jnp.exp(w_state - m_loc[cc, h])
    for cc, h in pairs:
        e_col[cc, h] = to_col(e_row[cc, h])
        b_col[cc, h] = to_col(b_row[cc, h])

    m_prev, a_dec, b_in = {}, {}, {}
    for h in range(H):
        m = m_ref[h:h + 1, 0:1]
        for cc in range(NC):
            m_prev[cc, h] = m
            m_new = jnp.maximum(g_tot[cc, h] + m, m_loc[cc, h])
            a_dec[cc, h] = jnp.exp(g_tot[cc, h] + m - m_new)
            b_in[cc, h] = jnp.exp(m_loc[cc, h] - m_new)
            m = m_new
        m_ref[h:h + 1, :] = jnp.broadcast_to(m, (1, 128))

    qh, kh, qb, kb = {}, {}, {}, {}
    for cc, h in pairs:
        qh[cc, h] = conv_silu(cc * L, h * Dh) * (Dh ** -0.5)
        kh[cc, h] = conv_silu(cc * L, W + h * Dh)
        qb[cc, h] = qh[cc, h].astype(BF16)
        kb[cc, h] = kh[cc, h].astype(BF16)

    raw = {}
    for cc, h in pairs:
        raw[cc, h] = lax.dot_general(qb[cc, h], kb[cc, h], (((1,), (1,)), ((), ())),
                                     preferred_element_type=F32)
    inter, sv, den0, floor = {}, {}, {}, {}
    for cc, h in pairs:
        dm = jnp.where(tril, b_col[cc, h] + rowterm[cc, h], NEG_BIG)
        m_intra = jnp.max(dm, axis=1, keepdims=True)
        m_inter = b_col[cc, h] + m_prev[cc, h]
        m_t = jnp.maximum(m_inter, m_intra)
        sqk = raw[cc, h] * jnp.exp(dm - m_t)
        inter[cc, h] = jnp.exp(m_inter - m_t)
        floor[cc, h] = jnp.exp(-m_t)
        den0[cc, h] = jnp.sum(sqk, axis=1, keepdims=True)
        sv[cc, h] = jnp.dot(sqk.astype(BF16), v_ref[rows_of(cc), cols_of(h)],
                            preferred_element_type=F32)

    for cc in range(NC):
        c_loc, n_loc = {}, {}
        for h in range(H):
            ke = kh[cc, h] * e_col[cc, h]
            c_loc[h] = lax.dot_general(ke.astype(BF16), v_ref[rows_of(cc), cols_of(h)],
                                       (((0,), (0,)), ((), ())), preferred_element_type=F32)
            n_loc[h] = jnp.sum(ke, axis=0, keepdims=True)
        for h in range(H):
            c_prev = c_ref[h]
            n_prev = n_ref[h:h + 1, :]
            q_c = jnp.dot(qb[cc, h], c_prev.astype(BF16), preferred_element_type=F32)
            q_n = jnp.sum(qh[cc, h] * n_prev, axis=1, keepdims=True)
            num = sv[cc, h] + inter[cc, h] * q_c
            den = den0[cc, h] + inter[cc, h] * q_n
            hh = num / jnp.maximum(jnp.abs(den), floor[cc, h])
            out_ref[rows_of(cc), cols_of(h)] = (
                o_ref[rows_of(cc), cols_of(h)].astype(F32) * hh).astype(BF16)
            c_ref[h] = a_dec[cc, h] * c_prev + b_in[cc, h] * c_loc[h]
            n_ref[h:h + 1, :] = a_dec[cc, h] * n_prev + b_in[cc, h] * n_loc[h]

    ubuf[0:L, :] = ubuf[R:R + L, :]


def _mlstm(P, gates, conv_w, B, S):
    T = B * S
    L, H, Dh = M_CHUNK, M_HEADS, M_DH
    R = MLSTM_CHUNKS_PER_STEP * L
    ns = S // R
    W = H * Dh
    blk = lambda cb: pl.BlockSpec((R, W), lambda b, c: (b * ns + c, cb))
    t_idx = jnp.arange(CONV_W * L) % L
    s_idx = jnp.arange(CONV_W * L) // L
    shift = (jnp.arange(2 * L)[None, :] == (L + t_idx - s_idx)[:, None]).astype(BF16)
    return pl.pallas_call(
        _mlstm_kernel,
        grid=(B, ns),
        in_specs=[blk(COL_MQ // W), blk(COL_MK // W), blk(COL_MV // W), blk(COL_MO // W),
                  pl.BlockSpec((GATE_ROWS, R), lambda b, c: (0, b * ns + c)),
                  pl.BlockSpec((CONV_W, 2 * W), lambda b, c: (0, 0)),
                  pl.BlockSpec((CONV_W * L, 2 * L), lambda b, c: (0, 0))],
        out_specs=pl.BlockSpec((R, W), lambda b, c: (b * ns + c, 0)),
        out_shape=jax.ShapeDtypeStruct((T, W), BF16),
        scratch_shapes=[pltpu.VMEM((L + R, 2 * W), BF16),
                        pltpu.VMEM((H, Dh, Dh), F32),
                        pltpu.VMEM((8, Dh), F32),
                        pltpu.VMEM((8, 128), F32)],
        compiler_params=_params(2),
        name="mlstm",
    )(P, P, P, P, gates, conv_w, shift)


def _attn_kernel(*refs, lam_init, online):
    if online:
        q_ref, k_ref, v_ref, lam_ref, gd_ref, out_ref, qs_ref, vt_ref, l_ref, acc_ref, m_ref = refs
    else:
        c_ref, q_ref, k_ref, v_ref, lam_ref, gd_ref, out_ref, qs_ref, vt_ref, l_ref, acc_ref = refs
    t = ATTN_T
    nq = q_ref.shape[0] // t
    nbmax = ATTN_BLOCKS_PER_TRIP

    lp = lam_ref[...]
    lam = (jnp.exp(jnp.sum(lp[0:1, :] * lp[1:2, :], axis=1, keepdims=True))
           - jnp.exp(jnp.sum(lp[2:3, :] * lp[3:4, :], axis=1, keepdims=True)) + lam_init)
    gd = gd_ref[...] * (1.0 - lam_init)

    for c in range(nq):
        vt_ref[:, c * t:(c + 1) * t] = v_ref[c * t:(c + 1) * t, :].astype(F32).T.astype(BF16)

    def step(qi, j, nb, diag_last):
        k = k_ref[j * t:(j + nb) * t, :]
        vt = vt_ref[:, j * t:(j + nb) * t]
        s = jnp.dot(k, qs_ref[qi], preferred_element_type=F32)
        if diag_last:
            row = lax.broadcasted_iota(jnp.int32, (nb * t, 2 * t), 0) - (nb - 1) * t
            col = lax.broadcasted_iota(jnp.int32, (nb * t, 2 * t), 1)
            qcol = jnp.where(col >= t, col - t, col)
            s = jnp.where(row <= qcol, s, NEG_BIG)
        if online:
            m_prev = m_ref[qi]
            m_new = jnp.maximum(m_prev, jnp.max(s, axis=0, keepdims=True))
            alpha = jnp.exp2(m_prev - m_new)
            p = jnp.exp2(s - m_new)
            l_ref[qi] = alpha * l_ref[qi] + jnp.sum(p, axis=0, keepdims=True)
            acc_ref[qi] = alpha * acc_ref[qi] + jnp.dot(vt, p.astype(BF16),
                                                        preferred_element_type=F32)
            m_ref[qi] = m_new
        else:
            p = jnp.exp2(s - c_ref[0])
            l_ref[qi] += jnp.sum(p, axis=0, keepdims=True)
            acc_ref[qi] += jnp.dot(vt, p.astype(BF16), preferred_element_type=F32)

    for qi in range(nq):
        q_t = q_ref[qi * t:(qi + 1) * t, :].astype(F32).T
        sub = lax.broadcasted_iota(jnp.int32, (128, t), 0)
        qs_ref[qi, :, 0:t] = jnp.where(sub < A_DK, q_t, 0.0).astype(BF16)
        qs_ref[qi, :, t:2 * t] = jnp.where(sub >= A_DK, q_t, 0.0).astype(BF16)
        l_ref[qi] = jnp.zeros((1, 2 * t), F32)
        acc_ref[qi] = jnp.zeros((128, 2 * t), F32)
        if online:
            m_ref[qi] = jnp.full((1, 2 * t), NEG_BIG, F32)

        j = 0
        while qi + 1 - j > nbmax:
            step(qi, j, nbmax, False)
            j += nbmax
        step(qi, j, qi + 1 - j, True)

        accn = acc_ref[qi] / l_ref[qi]
        o_t = accn[:, 0:t] - lam * accn[:, t:2 * t]
        ms = jnp.mean(o_t * o_t, axis=0, keepdims=True)
        o = (o_t * lax.rsqrt(ms + NORM_EPS)).T
        out_ref[qi * t:(qi + 1) * t, :] = (o * gd).astype(BF16)


def _attn(c, P, lam_p, gd, *, lam_init, B, S, online):
    T = B * S
    t = ATTN_T
    nq = S // t
    in_specs = [pl.BlockSpec((S, 128), lambda b, h: (b, COL_AQ // 128 + h)),
                pl.BlockSpec((S, 128), lambda b, h: (b, COL_AK // 128 + h)),
                pl.BlockSpec((S, 128), lambda b, h: (b, COL_AV // 128 + h)),
                pl.BlockSpec((4, A_DK), lambda b, h: (0, 0)),
                pl.BlockSpec((1, A_DV), lambda b, h: (0, 0))]
    scratch = [pltpu.VMEM((nq, 128, 2 * t), BF16),
               pltpu.VMEM((128, S), BF16),
               pltpu.VMEM((nq, 1, 2 * t), F32),
               pltpu.VMEM((nq, 128, 2 * t), F32)]
    args = (P, P, P, lam_p, gd)
    if online:
        scratch = scratch + [pltpu.VMEM((nq, 1, 2 * t), F32)]
    else:
        in_specs = [pl.BlockSpec(memory_space=pltpu.SMEM)] + in_specs
        args = (c,) + args
    return pl.pallas_call(
        functools.partial(_attn_kernel, lam_init=lam_init, online=online),
        grid=(B, A_HEADS),
        in_specs=in_specs,
        out_specs=pl.BlockSpec((S, 128), lambda b, h: (b, h)),
        out_shape=jax.ShapeDtypeStruct((T, A_HEADS * A_DV), BF16),
        scratch_shapes=scratch,
        compiler_params=_params(2),
        name="attn_online" if online else "attn",
    )(*args)


def _merge_kernel(hm_ref, pu_ref, ao_ref, gm_ref, gp_ref, ga_ref, x_ref, wm_ref, wp_ref, ps_ref,
                  wd_ref, wo_ref, bm_ref, bh_ref, out_ref, halo_ref, mrg_ref, win_ref,
                  *, tiles_per_seq):
    tm = x_ref.shape[0]
    G = POOL_G
    i = pl.program_id(0)
    tile_in_seq = i % tiles_per_seq

    @pl.when(tile_in_seq == 0)
    def _():
        halo_ref[...] = jnp.zeros(halo_ref.shape, BF16)

    pos1 = tile_in_seq * tm + lax.broadcasted_iota(jnp.int32, (tm, 1), 0) + 1

    tb = bm_ref.shape[1]
    n_win = len(POOL_WINDOWS)
    for r in range(tm // tb):
        for gi in range(n_win):
            cols = slice(gi * G, (gi + 1) * G)
            prev = halo_ref[:, cols] if r == 0 else pu_ref[r * tb - POOL_HALO:r * tb, cols]
            win = win_ref.at[r * n_win + gi]
            win[...] = jnp.dot(bm_ref[gi], pu_ref[r * tb:(r + 1) * tb, cols],
                               preferred_element_type=F32)
            win[0:POOL_HALO, :] += jnp.dot(bh_ref[gi], prev, preferred_element_type=F32)

    y_m = jnp.dot(hm_ref[...], wm_ref[...], preferred_element_type=F32)
    y_a = jnp.dot(ao_ref[...], wd_ref[...], preferred_element_type=F32)
    mrg_ref[...] = gm_ref[...].astype(F32) * y_m + ga_ref[...].astype(F32) * y_a

    for r in range(tm // tb):
        rows = slice(r * tb, (r + 1) * tb)
        for gi, w in enumerate(POOL_WINDOWS):
            cols = slice(gi * G, (gi + 1) * G)
            cnt = jnp.minimum(pos1[rows], w).astype(F32)
            pooled = win_ref[r * n_win + gi] / cnt - pu_ref[rows, cols].astype(F32)
            y_p = (jnp.dot(pooled.astype(BF16), wp_ref[gi], preferred_element_type=F32)
                   * ps_ref[:, cols])
            mrg_ref[rows, cols] += gp_ref[rows, cols].astype(F32) * y_p
    halo_ref[...] = pu_ref[tm - POOL_HALO:tm, :]

    out_ref[...] = x_ref[...] + jnp.dot(mrg_ref[...].astype(BF16), wo_ref[...],
                                        preferred_element_type=F32)


def _pool_bands(tm):
    t_idx = jnp.arange(tm)[:, None]
    main, halo = [], []
    for w in POOL_WINDOWS:
        d_main = t_idx - jnp.arange(tm)[None, :]
        d_halo = t_idx[:POOL_HALO] - (jnp.arange(POOL_HALO)[None, :] - POOL_HALO)
        main.append((d_main >= 0) & (d_main < w))
        halo.append((d_halo >= 0) & (d_halo < w))
    return jnp.stack(main).astype(BF16), jnp.stack(halo).astype(BF16)


def _merge(hm, P, ao, xf, wm, wp, ps, wd, wo, B, S, l):
    T = B * S
    tm = MERGE_TM
    tb = POOL_BAND_ROWS
    band_main, band_halo = _pool_bands(tb)
    row = lambda w, cb: pl.BlockSpec((tm, w), lambda i: (i, cb))
    full = lambda shape: pl.BlockSpec(shape, lambda i: (0,) * len(shape),
                                      pipeline_mode=pl.Buffered(1))
    layer = lambda shape: pl.BlockSpec((None,) + shape, lambda i: (l,) + (0,) * len(shape),
                                       pipeline_mode=pl.Buffered(1))
    return pl.pallas_call(
        functools.partial(_merge_kernel, tiles_per_seq=S // tm),
        grid=(T // tm,),
        in_specs=[row(D_MODEL, 0), row(D_MODEL, COL_PU // D_MODEL), row(D_MODEL, 0),
                  row(D_MODEL, COL_GATE // D_MODEL), row(D_MODEL, COL_GATE // D_MODEL + 1),
                  row(D_MODEL, COL_GATE // D_MODEL + 2), row(D_MODEL, 0),
                  layer((D_MODEL, D_MODEL)), layer((4, POOL_G, POOL_G)), full((1, D_MODEL)),
                  layer((D_MODEL, D_MODEL)), layer((D_MODEL, D_MODEL)),
                  full((4, tb, tb)), full((4, POOL_HALO, POOL_HALO))],
        out_specs=row(D_MODEL, 0),
        out_shape=jax.ShapeDtypeStruct((T, D_MODEL), F32),
        scratch_shapes=[pltpu.VMEM((POOL_HALO, D_MODEL), BF16),
                        pltpu.VMEM((tm, D_MODEL), F32),
                        pltpu.VMEM((tm // tb * len(POOL_WINDOWS), tb, POOL_G), F32)],
        compiler_params=_params(1),
        name="merge",
    )(hm, P, ao, P, P, P, xf, wm, wp, ps, wd, wo, band_main, band_halo)


def _ffn_kernel(x_ref, g_ref, wgu_ref, wdn_ref, out_ref, act_ref):
    x = x_ref[...]
    ms = jnp.mean(x * x, axis=-1, keepdims=True)
    hb = (x * lax.rsqrt(ms + NORM_EPS) * g_ref[...]).astype(BF16)
    for lo, hi in FFN_CHUNKS:
        gate = jnp.dot(hb, wgu_ref[:, lo:hi], preferred_element_type=F32)
        up = jnp.dot(hb, wgu_ref[:, FF + lo:FF + hi], preferred_element_type=F32)
        act_ref[:, lo:hi] = (gate * _sigmoid(gate) * up).astype(BF16)
    out_ref[...] = x + jnp.dot(act_ref[...], wdn_ref[...], preferred_element_type=F32)


def _ffn(xf, g, wgu, wdn, l):
    T = xf.shape[0]
    tm = FFN_TM
    layer = lambda shape: pl.BlockSpec((None,) + shape, lambda i: (l, 0, 0),
                                       pipeline_mode=pl.Buffered(1))
    return pl.pallas_call(
        _ffn_kernel,
        grid=(T // tm,),
        in_specs=[pl.BlockSpec((tm, D_MODEL), lambda i: (i, 0)),
                  pl.BlockSpec((1, D_MODEL), lambda i: (0, 0)),
                  layer((D_MODEL, 2 * FF)), layer((FF, D_MODEL))],
        out_specs=pl.BlockSpec((tm, D_MODEL), lambda i: (i, 0)),
        out_shape=jax.ShapeDtypeStruct((T, D_MODEL), F32),
        scratch_shapes=[pltpu.VMEM((tm, FF), BF16)],
        compiler_params=_params(1),
        name="ffn",
    )(xf, g, wgu, wdn)


def _rope_tables(S):
    half = A_DK // 2
    inv = ROPE_THETA ** (-jnp.arange(half, dtype=F32) / half)
    ang = jnp.arange(S, dtype=F32)[:, None] * inv[None, :]
    cos, sin = jnp.cos(ang), jnp.sin(ang)
    cos_t = jnp.concatenate([cos, cos, cos, cos], axis=1)
    sin_t = jnp.concatenate([-sin, sin, -sin, sin], axis=1)
    return cos_t, sin_t


def kernel(x, g_mix, w_in, b_if, conv_qk, w_m_out, w_pool, pool_scale, g_qk, lam_p, g_diff_head,
           w_diff_out, w_out, g_ffn, w_gate_up, w_down):
    B, S, _ = x.shape
    T = B * S
    xf = x.reshape(T, D_MODEL)
    cos_t, sin_t = _rope_tables(S)
    lane = jnp.arange(256)
    gmat = (lane[:, None] // A_DK == lane[None, :] // A_DK).astype(BF16)
    n_gate = 2 * M_HEADS
    gate_lo = 4 * M_HEADS * M_DH

    w_a = w_in[:, :, :gate_lo].astype(BF16)
    w_b = w_in[:, :, gate_lo + n_gate:].astype(BF16)
    wif_t = jnp.pad(jnp.swapaxes(w_in[:, :, gate_lo:gate_lo + n_gate], 1, 2),
                    ((0, 0), (0, GATE_ROWS - n_gate), (0, 0))).astype(BF16)
    wm, wp, wd, wo = (w.astype(BF16) for w in (w_m_out, w_pool, w_diff_out, w_out))
    wgu, wdn = w_gate_up.astype(BF16), w_down.astype(BF16)

    for l in range(DEPTH):
        bif = jnp.pad(b_if[l], (0, GATE_ROWS - n_gate)).reshape(GATE_ROWS, 1)
        g2 = jnp.concatenate([g_qk[l], g_qk[l]], axis=1)
        lam_init = 0.8 - 0.6 * math.exp(-0.3 * l)

        P, gates = _proj(xf, g_mix[l].reshape(1, D_MODEL), w_a, w_b, wif_t, bif, cos_t, sin_t,
                         g2, gmat, S, l)
        hm = _mlstm(P, gates, conv_qk[l], B, S)
        c = (A_DK ** 0.5 * math.log2(math.e)) * jnp.max(jnp.abs(g_qk[l][0])) * jnp.max(jnp.abs(g_qk[l][1]))
        attn_args = (c.reshape(1), P, lam_p[l], g_diff_head[l].reshape(1, A_DV))
        ao = lax.cond(c < ATTN_MAX_STATIC_BOUND,
                      functools.partial(_attn, lam_init=lam_init, B=B, S=S, online=False),
                      functools.partial(_attn, lam_init=lam_init, B=B, S=S, online=True),
                      *attn_args)
        x1 = _merge(hm, P, ao, xf, wm, wp, pool_scale[l].reshape(1, D_MODEL), wd, wo, B, S, l)
        xf = _ffn(x1, g_ffn[l].reshape(1, D_MODEL), wgu, wdn, l)
    return xf.reshape(B, S, D_MODEL)
```
